```python
import math, functools
import jax, jax.numpy as jnp
from jax import lax
import numpy as np

D_MODEL = 1024
BATCH = 8
SEQ = 2048
DEPTH = 2

CHUNK = 64
Q_BLOCK = 2 * CHUNK
HEAD_DIM = 64
N_HEADS = D_MODEL // HEAD_DIM
D_PLE = 256
D_FF = 256 * ((8 * D_MODEL // 3 + 255) // 256)
N_EXPERTS = 8
TOP_K = 2
D_FF_EXPERT = 7 * D_MODEL // 2
EXPERT_BLOCK = 256
RMS_EPS = 1e-6

kernel_name = 'yoco_fox_stickbreaking_moe_trunk'


def rmsnorm(x, g):
    xf = x.astype(jnp.float32)
    y = xf * lax.rsqrt(jnp.mean(xf * xf, axis=-1, keepdims=True) + RMS_EPS)
    return (y * g.astype(jnp.float32)).astype(x.dtype)


def split_heads(t):
    b, s, _ = t.shape
    return t.reshape(b, s, N_HEADS, HEAD_DIM).transpose(0, 2, 1, 3)


def to_query_blocks(t):
    b, h, s = t.shape[:3]
    nq = s // Q_BLOCK
    t = t.reshape((b, h, nq, Q_BLOCK) + t.shape[3:])
    return jnp.moveaxis(t, 2, 0)


def merge_query_blocks(o):
    nq, b, h, qb, d = o.shape
    return o.transpose(1, 0, 3, 2, 4).reshape(b, nq * qb, h * d)


def forgetting_attention(a, w_in, b_f, w_o):
    b, s, d_model = a.shape
    proj = a @ w_in
    q, k, v, f = jnp.split(proj, [d_model, 2 * d_model, 3 * d_model], axis=-1)
    log_f = jax.nn.log_sigmoid((f + b_f).astype(jnp.float32))
    c = jnp.cumsum(log_f, axis=1).transpose(0, 2, 1)
    q = split_heads(q) * (1.0 / math.sqrt(HEAD_DIM))
    k = split_heads(k)
    v = split_heads(v)
    kpos = jnp.arange(s)
    qpos = kpos.reshape(s // Q_BLOCK, Q_BLOCK)

    def block(args):
        qb, cb, qp = args
        logits = jnp.einsum('bhqd,bhkd->bhqk', qb, k).astype(jnp.float32)
        logits = logits + cb[..., :, None] - c[:, :, None, :]
        logits = jnp.where(kpos[None, :] <= qp[:, None], logits, -jnp.inf)
        probs = jax.nn.softmax(logits, axis=-1).astype(v.dtype)
        return jnp.einsum('bhqk,bhkd->bhqd', probs, v)

    o = lax.map(block, (to_query_blocks(q), to_query_blocks(c), qpos))
    return merge_query_blocks(o) @ w_o


def stick_breaking_attention(a, w_q, k, v, w_o):
    s = a.shape[1]
    q = split_heads(a @ w_q) * (1.0 / math.sqrt(HEAD_DIM))
    kpos = jnp.arange(s)
    qpos = kpos.reshape(s // Q_BLOCK, Q_BLOCK)

    def block(args):
        qb, qp = args
        z = jnp.einsum('bhqd,bhkd->bhqk', qb, k).astype(jnp.float32)
        mask = kpos[None, :] < qp[:, None]
        log_1mb = jnp.where(mask, jax.nn.log_sigmoid(-z), 0.0)
        later = lax.cumsum(log_1mb, axis=3, reverse=True) - log_1mb
        weights = jnp.where(mask, jnp.exp(jax.nn.log_sigmoid(z) + later), 0.0)
        return jnp.einsum('bhqk,bhkd->bhqd', weights.astype(v.dtype), v)

    o = lax.map(block, (to_query_blocks(q), qpos))
    return merge_query_blocks(o) @ w_o


def swiglu(h, w_gu, w_down):
    g, u = jnp.split(h @ w_gu, 2, axis=-1)
    return (jax.nn.silu(g) * u) @ w_down


def moe_swiglu(h, router_w, router_b, w_gu, w_down):
    b, s, d = h.shape
    t = b * s
    xt = h.reshape(t, d)
    logits = (xt @ router_w).astype(jnp.float32) + router_b.astype(jnp.float32)
    top_v, top_i = lax.top_k(logits, TOP_K)
    gates = jax.nn.softmax(top_v, axis=-1)
    n_assign = t * TOP_K
    flat_e = top_i.reshape(-1)
    flat_g = gates.reshape(-1)
    flat_tok = jnp.arange(n_assign) // TOP_K
    order = jnp.argsort(flat_e)
    sorted_e = flat_e[order]
    counts = jnp.bincount(flat_e, length=N_EXPERTS)
    padded = (counts + EXPERT_BLOCK - 1) // EXPERT_BLOCK * EXPERT_BLOCK
    start = jnp.cumsum(counts) - counts
    pstart = jnp.cumsum(padded) - padded
    dest = pstart[sorted_e] + jnp.arange(n_assign) - start[sorted_e]
    n_blocks = -(-n_assign // EXPERT_BLOCK) + N_EXPERTS
    n_rows = n_blocks * EXPERT_BLOCK
    row_tok = jnp.zeros((n_rows,), jnp.int32).at[dest].set(flat_tok[order])
    row_gate = jnp.zeros((n_rows,), jnp.float32).at[dest].set(flat_g[order])
    block_e = jnp.minimum(
        jnp.searchsorted(jnp.cumsum(padded), jnp.arange(n_blocks) * EXPERT_BLOCK, side='right'),
        N_EXPERTS - 1)
    xb = xt[row_tok].reshape(n_blocks, EXPERT_BLOCK, d)

    def expert_block(args):
        xe, e = args
        return swiglu(xe, w_gu[e], w_down[e])

    yb = lax.map(expert_block, (xb, block_e)).reshape(n_rows, d)
    out = jnp.zeros((t, d), h.dtype).at[row_tok].add(yb * row_gate[:, None].astype(yb.dtype))
    return out.reshape(b, s, d)


def setup_inputs(seed: int = 0) -> dict:
    key = jax.random.key(seed)
    ks = jax.random.split(key, 24)
    n_a = DEPTH // 2
    n_b = DEPTH - n_a
    n_dense = (DEPTH + 1) // 2
    n_moe = DEPTH // 2
    D, H, F, FE = D_MODEL, N_HEADS, D_FF, D_FF_EXPERT

    def nrm(k, shape, scale):
        return jax.random.normal(k, shape, jnp.float32) * scale

    return {
        'x': nrm(ks[0], (BATCH, SEQ, D), 1.0),
        'p': nrm(ks[1], (DEPTH, BATCH, SEQ, D_PLE), 1.0),
        'attn_norm': 1.0 + nrm(ks[2], (DEPTH, D), 0.02),
        'ffn_norm': 1.0 + nrm(ks[3], (DEPTH, D), 0.02),
        'w_in_a': nrm(ks[4], (n_a, D, 3 * D + H), D ** -0.5),
        'b_f': jax.random.uniform(ks[5], (n_a, H), jnp.float32, 2.0, 5.0),
        'w_o_a': nrm(ks[6], (n_a, D, D), D ** -0.5),
        'kv_norm': 1.0 + nrm(ks[7], (D,), 0.02),
        'w_kv': nrm(ks[8], (D, 2 * D), D ** -0.5),
        'w_q_b': nrm(ks[9], (n_b, D, D), D ** -0.5),
        'w_o_b': nrm(ks[10], (n_b, D, D), D ** -0.5),
        'w_gu_dense': nrm(ks[11], (n_dense, D, 2 * F), D ** -0.5),
        'w_down_dense': nrm(ks[12], (n_dense, F, D), F ** -0.5),
        'router_w': nrm(ks[13], (n_moe, D, N_EXPERTS), D ** -0.5),
        'router_b': nrm(ks[14], (n_moe, N_EXPERTS), 0.01),
        'w_gu_moe': nrm(ks[15], (n_moe, N_EXPERTS, D, 2 * FE), D ** -0.5),
        'w_down_moe': nrm(ks[16], (n_moe, N_EXPERTS, FE, D), FE ** -0.5),
        'w_ple_proj': nrm(ks[17], (DEPTH, D_PLE, D), D_PLE ** -0.5),
        'w_ple_gate': nrm(ks[18], (DEPTH, D, D), D ** -0.5),
        'final_norm': 1.0 + nrm(ks[19], (D,), 0.02),
    }


def reference(x, p, attn_norm, ffn_norm, w_in_a, b_f, w_o_a, kv_norm, w_kv, w_q_b, w_o_b,
              w_gu_dense, w_down_dense, router_w, router_b, w_gu_moe, w_down_moe,
              w_ple_proj, w_ple_gate, final_norm):
    n_a = DEPTH // 2
    h = x
    k_shared = None
    v_shared = None
    for i in range(DEPTH):
        a = rmsnorm(h, attn_norm[i])
        if i < n_a:
            h = h + forgetting_attention(a, w_in_a[i], b_f[i], w_o_a[i])
        else:
            j = i - n_a
            h = h + stick_breaking_attention(a, w_q_b[j], k_shared, v_shared, w_o_b[j])
        f = rmsnorm(h, ffn_norm[i])
        if i % 2 == 0:
            h = h + swiglu(f, w_gu_dense[i // 2], w_down_dense[i // 2])
        else:
            h = h + moe_swiglu(f, router_w[i // 2], router_b[i // 2], w_gu_moe[i // 2], w_down_moe[i // 2])
        h = h + (p[i] @ w_ple_proj[i]) * jax.nn.sigmoid(h @ w_ple_gate[i])
        if i == n_a - 1:
            kv = rmsnorm(h, kv_norm) @ w_kv
            k_s, v_s = jnp.split(kv, 2, axis=-1)
            k_shared = split_heads(k_s)
            v_shared = split_heads(v_s)
    return rmsnorm(h, final_norm)
```

```python
import functools
import math

import jax
import jax.numpy as jnp
from jax import lax
from jax.experimental import pallas as pl
from jax.experimental.pallas import tpu as pltpu

HEAD_DIM = 64
LANES = 128
HEADS_PER_BLOCK = LANES // HEAD_DIM
N_EXPERTS = 8
TOP_K = 2
RMS_EPS = 1e-6
ATTN_TILE = 256
CUMSUM_CHUNK = 256
EXPERT_TILE = 512
VMEM_LIMIT = 56 * 1024 * 1024

BF16 = jnp.bfloat16
F32 = jnp.float32


def _params(*semantics):
    return pltpu.CompilerParams(dimension_semantics=semantics, vmem_limit_bytes=VMEM_LIMIT)


def _dot(a, b):
    return jnp.dot(a, b, preferred_element_type=F32)


def _dot_nt(a, b):
    return lax.dot_general(a, b, (((1,), (1,)), ((), ())), preferred_element_type=F32)


def _rstd(x):
    return lax.rsqrt(jnp.mean(x * x, axis=-1, keepdims=True) + RMS_EPS)


def _log_sigmoid(u):
    return jnp.minimum(u, 0.0) - jnp.log1p(jnp.exp(-jnp.abs(u)))


def _split3(x):
    hi = x.astype(BF16)
    r1 = x - hi.astype(F32)
    mid = r1.astype(BF16)
    lo = (r1 - mid.astype(F32)).astype(BF16)
    return hi, mid, lo


def _qkvf_kernel(x_ref, g_ref, w_ref, wf_ref, bf_ref, qkv_ref, lf_ref, xn_ref, *, n_q_blocks):
    j = pl.program_id(1)

    @pl.when(j == 0)
    def _():
        x = x_ref[...]
        xn = (x * _rstd(x) * g_ref[...]).astype(BF16)
        xn_ref[...] = xn
        f = _dot_nt(wf_ref[...], xn) + bf_ref[...]
        lf_ref[...] = _log_sigmoid(f)

    scale = jnp.where(j < n_q_blocks, 1.0 / math.sqrt(HEAD_DIM), 1.0).astype(F32)
    qkv_ref[...] = (_dot(xn_ref[...], w_ref[...]) * scale).astype(BF16)


def _qkvf_proj(x, g, w_qkv, w_f_t, b_f, *, tm=512, tn=1024):
    t, d = x.shape
    n = w_qkv.shape[1]
    h = w_f_t.shape[0]
    return pl.pallas_call(
        functools.partial(_qkvf_kernel, n_q_blocks=d // tn),
        grid=(t // tm, n // tn),
        in_specs=[
            pl.BlockSpec((tm, d), lambda i, j: (i, 0)),
            pl.BlockSpec((1, d), lambda i, j: (0, 0)),
            pl.BlockSpec((d, tn), lambda i, j: (0, j)),
            pl.BlockSpec((h, d), lambda i, j: (0, 0)),
            pl.BlockSpec((h, 1), lambda i, j: (0, 0)),
        ],
        out_specs=[
            pl.BlockSpec((tm, tn), lambda i, j: (i, j)),
            pl.BlockSpec((h, tm), lambda i, j: (0, i)),
        ],
        out_shape=[
            jax.ShapeDtypeStruct((t, n), BF16),
            jax.ShapeDtypeStruct((h, t), F32),
        ],
        scratch_shapes=[pltpu.VMEM((tm, d), BF16)],
        compiler_params=_params("parallel", "arbitrary"),
        name="qkvf_proj",
    )(x, g, w_qkv, w_f_t, b_f)


def _cumsum_kernel(lf_ref, tri_ref, c_ref):
    h, s = lf_ref.shape
    tri = tri_ref[...]
    carry = jnp.zeros((h, 1), F32)
    for c0 in range(0, s, CUMSUM_CHUNK):
        hi, mid, lo = _split3(lf_ref[:, c0:c0 + CUMSUM_CHUNK])
        cs = _dot(lo, tri) + _dot(mid, tri) + _dot(hi, tri) + carry
        for hd in range(h):
            c_ref[hd, :, c0:c0 + CUMSUM_CHUNK] = cs[hd:hd + 1, :]
        carry = cs[:, CUMSUM_CHUNK - 1:CUMSUM_CHUNK]


def _forget_cumsum(lf_t, tri, *, seq):
    h, t = lf_t.shape
    return pl.pallas_call(
        _cumsum_kernel,
        grid=(t // seq,),
        in_specs=[
            pl.BlockSpec((h, seq), lambda b: (0, b)),
            pl.BlockSpec((CUMSUM_CHUNK, CUMSUM_CHUNK), lambda b: (0, 0)),
        ],
        out_specs=pl.BlockSpec((h, 1, seq), lambda b: (0, 0, b)),
        out_shape=jax.ShapeDtypeStruct((h, 1, t), F32),
        compiler_params=_params("parallel"),
        name="forget_cumsum",
    )(lf_t, tri)


def _fox_kernel(q_ref, k_ref, v_ref, c_ref, o_ref, m_ref, l_ref, acc_ref):
    qi = pl.program_id(2)
    tq = q_ref.shape[0]
    q2 = q_ref[...]
    lane = lax.broadcasted_iota(jnp.int32, q2.shape, 1)
    zero = jnp.zeros_like(q2)
    q_heads = (jnp.where(lane < HEAD_DIM, q2, zero), jnp.where(lane >= HEAD_DIM, q2, zero))

    m_ref[...] = jnp.full(m_ref.shape, -jnp.inf, F32)
    l_ref[...] = jnp.zeros(l_ref.shape, F32)
    acc_ref[...] = jnp.zeros(acc_ref.shape, F32)

    def tile(kt, diagonal):
        k0 = pl.multiple_of(kt * tq, tq)
        k = k_ref[pl.ds(k0, tq), :]
        v = v_ref[pl.ds(k0, tq), :]
        for hd in range(HEADS_PER_BLOCK):
            s = _dot_nt(q_heads[hd], k) - c_ref[hd, :, pl.ds(k0, tq)]
            if diagonal:
                row = lax.broadcasted_iota(jnp.int32, s.shape, 0)
                col = lax.broadcasted_iota(jnp.int32, s.shape, 1)
                s = jnp.where(col <= row, s, -jnp.inf)
            m_prev = m_ref[hd]
            m_new = jnp.maximum(m_prev, jnp.max(s, axis=1, keepdims=True))
            alpha = jnp.exp(m_prev - m_new)
            p = jnp.exp(s - m_new)
            l_ref[hd] = alpha * l_ref[hd] + jnp.sum(p, axis=1, keepdims=True)
            acc_ref[hd] = alpha * acc_ref[hd] + _dot(p.astype(BF16), v)
            m_ref[hd] = m_new

    def body(kt, carry):
        tile(kt, False)
        return carry

    lax.fori_loop(0, qi, body, 0)
    tile(qi, True)

    out = [acc_ref[hd] / l_ref[hd] for hd in range(HEADS_PER_BLOCK)]
    lane_o = lax.broadcasted_iota(jnp.int32, out[0].shape, 1)
    o_ref[...] = jnp.where(lane_o < HEAD_DIM, out[0], out[1]).astype(BF16)


def _fox_attention(qkv, c, *, batch, seq, d_model):
    t = qkv.shape[0]
    n_pairs = d_model // LANES
    tq = ATTN_TILE
    nq = seq // tq
    return pl.pallas_call(
        _fox_kernel,
        grid=(batch, n_pairs, nq),
        in_specs=[
            pl.BlockSpec((tq, LANES), lambda b, hp, qi: (b * nq + qi, hp)),
            pl.BlockSpec((seq, LANES), lambda b, hp, qi: (b, n_pairs + hp)),
            pl.BlockSpec((seq, LANES), lambda b, hp, qi: (b, 2 * n_pairs + hp)),
            pl.BlockSpec((HEADS_PER_BLOCK, 1, seq), lambda b, hp, qi: (hp, 0, b)),
        ],
        out_specs=pl.BlockSpec((tq, LANES), lambda b, hp, qi: (b * nq + qi, hp)),
        out_shape=jax.ShapeDtypeStruct((t, d_model), BF16),
        scratch_shapes=[
            pltpu.VMEM((HEADS_PER_BLOCK, tq, 1), F32),
            pltpu.VMEM((HEADS_PER_BLOCK, tq, 1), F32),
            pltpu.VMEM((HEADS_PER_BLOCK, tq, LANES), F32),
        ],
        compiler_params=_params("parallel", "parallel", "arbitrary"),
        name="fox_attention",
    )(qkv, qkv, qkv, c)


def _sb_kernel(q_ref, k_ref, v_ref, tri_ref, o_ref, r_ref, acc_ref):
    qi = pl.program_id(2)
    tq = q_ref.shape[0]
    q2 = q_ref[...]
    lane = lax.broadcasted_iota(jnp.int32, q2.shape, 1)
    zero = jnp.zeros_like(q2)
    q_heads = (jnp.where(lane < HEAD_DIM, q2, zero), jnp.where(lane >= HEAD_DIM, q2, zero))
    tri = tri_ref[...]

    r_ref[...] = jnp.zeros(r_ref.shape, F32)
    acc_ref[...] = jnp.zeros(acc_ref.shape, F32)

    def tile(kt, diagonal):
        k0 = pl.multiple_of(kt * tq, tq)
        k = k_ref[pl.ds(k0, tq), :]
        v = v_ref[pl.ds(k0, tq), :]
        for hd in range(HEADS_PER_BLOCK):
            z = _dot_nt(q_heads[hd], k)
            lm = -(jnp.maximum(z, 0.0) + jnp.log1p(jnp.exp(-jnp.abs(z))))
            if diagonal:
                row = lax.broadcasted_iota(jnp.int32, z.shape, 0)
                col = lax.broadcasted_iota(jnp.int32, z.shape, 1)
                visible = col < row
                lm = jnp.where(visible, lm, 0.0)
            hi = lm.astype(BF16)
            lo = (lm - hi.astype(F32)).astype(BF16)
            p = _dot(lo, tri) + _dot(hi, tri)
            r = r_ref[hd]
            w = jnp.exp(z + p + r)
            if diagonal:
                w = jnp.where(visible, w, 0.0)
            acc_ref[hd] += _dot(w.astype(BF16), v)
            r_ref[hd] = r + p[:, 0:1]

    tile(qi, True)

    def body(i, carry):
        tile(qi - 1 - i, False)
        return carry

    lax.fori_loop(0, qi, body, 0)

    lane_o = lax.broadcasted_iota(jnp.int32, (tq, LANES), 1)
    o_ref[...] = jnp.where(lane_o < HEAD_DIM, acc_ref[0], acc_ref[1]).astype(BF16)


def _sb_attention(q, kv, tri, *, batch, seq, d_model):
    t = q.shape[0]
    n_pairs = d_model // LANES
    tq = ATTN_TILE
    nq = seq // tq
    return pl.pallas_call(
        _sb_kernel,
        grid=(batch, n_pairs, nq),
        in_specs=[
            pl.BlockSpec((tq, LANES), lambda b, hp, qi: (b * nq + qi, hp)),
            pl.BlockSpec((seq, LANES), lambda b, hp, qi: (b, hp)),
            pl.BlockSpec((seq, LANES), lambda b, hp, qi: (b, n_pairs + hp)),
            pl.BlockSpec((tq, tq), lambda b, hp, qi: (0, 0)),
        ],
        out_specs=pl.BlockSpec((tq, LANES), lambda b, hp, qi: (b * nq + qi, hp)),
        out_shape=jax.ShapeDtypeStruct((t, d_model), BF16),
        scratch_shapes=[
            pltpu.VMEM((HEADS_PER_BLOCK, tq, 1), F32),
            pltpu.VMEM((HEADS_PER_BLOCK, tq, LANES), F32),
        ],
        compiler_params=_params("parallel", "parallel", "arbitrary"),
        name="sb_attention",
    )(q, kv, kv, tri)


def _oproj_kernel(o_ref, w_ref, h_ref, g_ref, h1_ref, fn_ref):
    h1 = h_ref[...] + _dot(o_ref[...], w_ref[...])
    h1_ref[...] = h1
    fn_ref[...] = (h1 * _rstd(h1) * g_ref[...]).astype(fn_ref.dtype)


def _oproj_router_kernel(o_ref, w_ref, h_ref, g_ref, rw_ref, rb_ref, h1_ref, fn_ref, idx_ref, gate_ref):
    h1 = h_ref[...] + _dot(o_ref[...], w_ref[...])
    h1_ref[...] = h1
    fn = h1 * _rstd(h1) * g_ref[...]
    fn_ref[...] = fn
    logits = jnp.dot(fn, rw_ref[...], preferred_element_type=F32,
                     precision=lax.Precision.HIGHEST) + rb_ref[...]
    lane = lax.broadcasted_iota(jnp.int32, logits.shape, 1)
    logits = jnp.where(lane < N_EXPERTS, logits, -jnp.inf)
    lane_f = lane.astype(F32)
    m1 = jnp.max(logits, axis=1, keepdims=True)
    i1 = jnp.min(jnp.where(logits == m1, lane_f, float(LANES)), axis=1, keepdims=True)
    rest = jnp.where(lane_f == i1, -jnp.inf, logits)
    m2 = jnp.max(rest, axis=1, keepdims=True)
    i2 = jnp.min(jnp.where(rest == m2, lane_f, float(LANES)), axis=1, keepdims=True)
    e2 = jnp.exp(m2 - m1)
    g1 = 1.0 / (1.0 + e2)
    g2 = e2 / (1.0 + e2)
    idx_ref[...] = jnp.where(lane == 0, i1, jnp.where(lane == 1, i2, 0.0)).astype(jnp.int32)
    gate_ref[...] = jnp.where(lane == 0, g1, jnp.where(lane == 1, g2, 0.0))


def _oproj(o, w_o, h, g, router=None, *, tm=512):
    t, d = h.shape
    row = lambda i: (i, 0)
    const = lambda i: (0, 0)
    in_specs = [
        pl.BlockSpec((tm, d), row),
        pl.BlockSpec((d, d), const),
        pl.BlockSpec((tm, d), row),
        pl.BlockSpec((1, d), const),
    ]
    if router is None:
        return pl.pallas_call(
            _oproj_kernel,
            grid=(t // tm,),
            in_specs=in_specs,
            out_specs=[pl.BlockSpec((tm, d), row), pl.BlockSpec((tm, d), row)],
            out_shape=[jax.ShapeDtypeStruct((t, d), F32), jax.ShapeDtypeStruct((t, d), BF16)],
            compiler_params=_params("parallel"),
            name="oproj_norm",
        )(o, w_o, h, g)
    rw, rb = router
    return pl.pallas_call(
        _oproj_router_kernel,
        grid=(t // tm,),
        in_specs=in_specs + [pl.BlockSpec((d, LANES), const), pl.BlockSpec((1, LANES), const)],
        out_specs=[pl.BlockSpec((tm, d), row), pl.BlockSpec((tm, d), row),
                   pl.BlockSpec((tm, LANES), row), pl.BlockSpec((tm, LANES), row)],
        out_shape=[jax.ShapeDtypeStruct((t, d), F32), jax.ShapeDtypeStruct((t, d), F32),
                   jax.ShapeDtypeStruct((t, LANES), jnp.int32), jax.ShapeDtypeStruct((t, LANES), F32)],
        compiler_params=_params("parallel"),
        name="oproj_norm_router",
    )(o, w_o, h, g, rw, rb)


def _swiglu_chunk(x, wg, wu, wd):
    g = _dot(x, wg)
    u = _dot(x, wu)
    act = (g * (1.0 / (1.0 + jnp.exp(-g))) * u).astype(BF16)
    return _dot(act, wd)


def _dense_ffn_kernel(x_ref, wg_ref, wu_ref, wd_ref, h_ref, o_ref):
    @pl.when(pl.program_id(1) == 0)
    def _():
        o_ref[...] = h_ref[...]

    o_ref[...] += _swiglu_chunk(x_ref[...], wg_ref[...], wu_ref[...], wd_ref[...])


def _dense_ffn(xn, w_gu, w_down, h, *, tm=1024, tf=256):
    t, d = h.shape
    f = w_down.shape[0]
    nf = f // tf
    return pl.pallas_call(
        _dense_ffn_kernel,
        grid=(t // tm, nf),
        in_specs=[
            pl.BlockSpec((tm, d), lambda i, j: (i, 0)),
            pl.BlockSpec((d, tf), lambda i, j: (0, j)),
            pl.BlockSpec((d, tf), lambda i, j: (0, nf + j)),
            pl.BlockSpec((tf, d), lambda i, j: (j, 0)),
            pl.BlockSpec((tm, d), lambda i, j: (i, 0)),
        ],
        out_specs=pl.BlockSpec((tm, d), lambda i, j: (i, 0)),
        out_shape=jax.ShapeDtypeStruct((t, d), F32),
        compiler_params=_params("parallel", "arbitrary"),
        name="dense_ffn",
    )(xn, w_gu, w_gu, w_down, h)


def _ple(h2, p, wp, wg):
    gate = 1.0 / (1.0 + jnp.exp(-_dot(h2.astype(BF16), wg)))
    return h2 + _dot(p.astype(BF16), wp) * gate


def _ple_norm_kernel(h_ref, p_ref, wp_ref, wg_ref, gains_ref, h3_ref, *y_refs):
    h3 = _ple(h_ref[...], p_ref[...], wp_ref[...], wg_ref[...])
    h3_ref[...] = h3
    y = h3 * _rstd(h3)
    for n, y_ref in enumerate(y_refs):
        y_ref[...] = (y * gains_ref[n:n + 1, :]).astype(y_ref.dtype)


def _ple_norm(h2, p, wp, wg, gains, *, tm=512):
    t, d = h2.shape
    dp = p.shape[1]
    n_g = gains.shape[0]
    row = lambda i: (i, 0)
    const = lambda i: (0, 0)
    return pl.pallas_call(
        _ple_norm_kernel,
        grid=(t // tm,),
        in_specs=[
            pl.BlockSpec((tm, d), row),
            pl.BlockSpec((tm, dp), row),
            pl.BlockSpec((dp, d), const),
            pl.BlockSpec((d, d), const),
            pl.BlockSpec((n_g, d), const),
        ],
        out_specs=[pl.BlockSpec((tm, d), row)] * (1 + n_g),
        out_shape=[jax.ShapeDtypeStruct((t, d), F32)] + [jax.ShapeDtypeStruct((t, d), BF16)] * n_g,
        compiler_params=_params("parallel"),
        name="ple_norm",
    )(h2, p, wp, wg, gains)


def _matmul_kernel(x_ref, w_ref, o_ref, *, scale):
    o_ref[...] = (_dot(x_ref[...], w_ref[...]) * scale).astype(o_ref.dtype)


def _matmul(x, w, *, scale=1.0, tm=512, tn=1024):
    t, d = x.shape
    n = w.shape[1]
    return pl.pallas_call(
        functools.partial(_matmul_kernel, scale=scale),
        grid=(t // tm, n // tn),
        in_specs=[pl.BlockSpec((tm, d), lambda i, j: (i, 0)), pl.BlockSpec((d, tn), lambda i, j: (0, j))],
        out_specs=pl.BlockSpec((tm, tn), lambda i, j: (i, j)),
        out_shape=jax.ShapeDtypeStruct((t, n), BF16),
        compiler_params=_params("parallel", "parallel"),
        name="matmul",
    )(x, w)


def _row_copy(src_hbm, dst_hbm, sem, src_row, dst_row):
    return pltpu.make_async_copy(src_hbm.at[pl.ds(src_row, 1)], dst_hbm.at[pl.ds(dst_row, 1)], sem)


def _dispatch_kernel(dest_ref, x_hbm, xb_in_hbm, xb_hbm, sems, *, tm):
    del xb_in_hbm
    i = pl.program_id(0)
    n = pl.num_programs(0)
    slot = i % 2

    def wait_all(sem):
        pltpu.make_async_copy(x_hbm.at[pl.ds(0, tm * TOP_K)], xb_hbm.at[pl.ds(0, tm * TOP_K)], sem).wait()

    def issue(r, carry):
        tok = i * tm + r
        for k in range(TOP_K):
            _row_copy(x_hbm, xb_hbm, sems.at[slot], tok, dest_ref[0, 0, r * TOP_K + k]).start()
        return carry

    lax.fori_loop(0, tm, issue, 0, unroll=8)

    @pl.when(i > 0)
    def _():
        wait_all(sems.at[1 - slot])

    @pl.when(i == n - 1)
    def _():
        wait_all(sems.at[slot])


def _dispatch(x, dest3, xb_zero, *, tm):
    t, d = x.shape
    return pl.pallas_call(
        functools.partial(_dispatch_kernel, tm=tm),
        grid=(t // tm,),
        in_specs=[
            pl.BlockSpec((1, 1, tm * TOP_K), lambda i: (i, 0, 0), memory_space=pltpu.SMEM),
            pl.BlockSpec(memory_space=pl.ANY),
            pl.BlockSpec(memory_space=pl.ANY),
        ],
        out_specs=pl.BlockSpec(memory_space=pl.ANY),
        out_shape=jax.ShapeDtypeStruct(xb_zero.shape, xb_zero.dtype),
        scratch_shapes=[pltpu.SemaphoreType.DMA((2,))],
        input_output_aliases={2: 0},
        compiler_params=_params("arbitrary"),
        name="moe_dispatch",
    )(dest3, x, xb_zero)


def _expert_kernel(te_ref, tv_ref, x_ref, wg_ref, wu_ref, wd_ref, y_ref, xs_ref):
    ti = pl.program_id(0)
    j = pl.program_id(1)

    @pl.when(j == 0)
    def _():
        xs_ref[...] = x_ref[...].astype(BF16)
        y_ref[...] = jnp.zeros(y_ref.shape, F32)

    @pl.when(tv_ref[ti] > 0)
    def _():
        y_ref[...] += _swiglu_chunk(xs_ref[...], wg_ref[0], wu_ref[0], wd_ref[0])


def _experts(xb, w_gu, w_down, tile_expert, tile_valid, *, fc=512):
    n_rows, d = xb.shape
    fe = w_down.shape[1]
    nj = fe // fc
    n_tiles = n_rows // EXPERT_TILE

    def jj(ti, j, tv):
        return jnp.where(tv[ti] > 0, j, nj - 1)

    return pl.pallas_call(
        _expert_kernel,
        grid_spec=pltpu.PrefetchScalarGridSpec(
            num_scalar_prefetch=2,
            grid=(n_tiles, nj),
            in_specs=[
                pl.BlockSpec((EXPERT_TILE, d), lambda ti, j, te, tv: (ti, 0)),
                pl.BlockSpec((1, d, fc), lambda ti, j, te, tv: (te[ti], 0, jj(ti, j, tv))),
                pl.BlockSpec((1, d, fc), lambda ti, j, te, tv: (te[ti], 0, nj + jj(ti, j, tv))),
                pl.BlockSpec((1, fc, d), lambda ti, j, te, tv: (te[ti], jj(ti, j, tv), 0)),
            ],
            out_specs=pl.BlockSpec((EXPERT_TILE, d), lambda ti, j, te, tv: (ti, 0)),
            scratch_shapes=[pltpu.VMEM((EXPERT_TILE, d), BF16)],
        ),
        out_shape=jax.ShapeDtypeStruct((n_rows, d), F32),
        compiler_params=_params("parallel", "arbitrary"),
        name="moe_experts",
    )(tile_expert, tile_valid, xb, w_gu, w_gu, w_down)


def _combine_kernel(dest_ref, yb_hbm, h_ref, gate_ref, p_ref, wp_ref, wg_ref, g_ref, o_ref, y_ref, sem, *, tm):
    def issue(r, carry):
        for k in range(TOP_K):
            pltpu.make_async_copy(yb_hbm.at[pl.ds(dest_ref[0, 0, r * TOP_K + k], 1)],
                                  y_ref.at[k, pl.ds(r, 1)], sem).start()
        return carry

    lax.fori_loop(0, tm, issue, 0, unroll=8)
    for k in range(TOP_K):
        pltpu.make_async_copy(yb_hbm.at[pl.ds(0, tm)], y_ref.at[k], sem).wait()

    gates = gate_ref[...]
    h2 = h_ref[...] + (y_ref[0] * gates[:, 0:1] + y_ref[1] * gates[:, 1:2])
    h3 = _ple(h2, p_ref[...], wp_ref[...], wg_ref[...])
    o_ref[...] = h3 * _rstd(h3) * g_ref[...]


def _combine(dest3, yb, h, gates, p, wp, wg, g, *, tm):
    t, d = h.shape
    dp = p.shape[1]
    row = lambda i: (i, 0)
    const = lambda i: (0, 0)
    return pl.pallas_call(
        functools.partial(_combine_kernel, tm=tm),
        grid=(t // tm,),
        in_specs=[
            pl.BlockSpec((1, 1, tm * TOP_K), lambda i: (i, 0, 0), memory_space=pltpu.SMEM),
            pl.BlockSpec(memory_space=pl.ANY),
            pl.BlockSpec((tm, d), row),
            pl.BlockSpec((tm, LANES), row),
            pl.BlockSpec((tm, dp), row),
            pl.BlockSpec((dp, d), const),
            pl.BlockSpec((d, d), const),
            pl.BlockSpec((1, d), const),
        ],
        out_specs=pl.BlockSpec((tm, d), row),
        out_shape=jax.ShapeDtypeStruct((t, d), F32),
        scratch_shapes=[pltpu.VMEM((TOP_K, tm, d), F32), pltpu.SemaphoreType.DMA],
        compiler_params=_params("arbitrary"),
        name="moe_combine",
    )(dest3, yb, h, gates, p, wp, wg, g)


def _routing_tables(idx, n_tokens, *, tm):
    e_flat = idx.reshape(-1)
    onehot = (e_flat[:, None] == jnp.arange(N_EXPERTS, dtype=jnp.int32)[None, :]).astype(jnp.int32)
    csum = jnp.cumsum(onehot, axis=0)
    counts = csum[-1]
    padded = (counts + EXPERT_TILE - 1) // EXPERT_TILE * EXPERT_TILE
    pend = jnp.cumsum(padded)
    pstart = pend - padded
    dest = jnp.sum(onehot * (csum - 1 + pstart[None, :]), axis=1)
    n_tiles = (n_tokens * TOP_K) // EXPERT_TILE + N_EXPERTS
    tile_row0 = jnp.arange(n_tiles, dtype=jnp.int32) * EXPERT_TILE
    tile_expert = jnp.minimum(jnp.searchsorted(pend, tile_row0, side="right"), N_EXPERTS - 1).astype(jnp.int32)
    tile_valid = (tile_row0 < pend[-1]).astype(jnp.int32)
    dest3 = dest.astype(jnp.int32).reshape(n_tokens // tm, 1, tm * TOP_K)
    return dest3, tile_expert, tile_valid, n_tiles


def kernel(x, p, attn_norm, ffn_norm, w_in_a, b_f, w_o_a, kv_norm, w_kv, w_q_b, w_o_b, w_gu_dense, w_down_dense, router_w, router_b, w_gu_moe, w_down_moe, w_ple_proj, w_ple_gate, final_norm):
    batch, seq, d = x.shape
    t = batch * seq
    n_heads = d // HEAD_DIM
    inv_sqrt_d = 1.0 / math.sqrt(HEAD_DIM)
    x2 = x.reshape(t, d)
    p2 = p.reshape(p.shape[0], t, p.shape[-1])
    bf = lambda w: w.astype(BF16)

    ii = lax.broadcasted_iota(jnp.int32, (CUMSUM_CHUNK, CUMSUM_CHUNK), 0)
    jj = lax.broadcasted_iota(jnp.int32, (CUMSUM_CHUNK, CUMSUM_CHUNK), 1)
    tri_le = (ii <= jj).astype(BF16)
    tri_ge = (ii >= jj).astype(BF16)

    w_in = w_in_a[0]
    qkv, lf_t = _qkvf_proj(x2, attn_norm[0:1], bf(w_in[:, :3 * d]), bf(w_in[:, 3 * d:].T),
                           b_f[0].reshape(n_heads, 1))
    c = _forget_cumsum(lf_t, tri_le, seq=seq)
    o0 = _fox_attention(qkv, c, batch=batch, seq=seq, d_model=d)
    h1, fn0 = _oproj(o0, bf(w_o_a[0]), x2, ffn_norm[0:1])
    h2 = _dense_ffn(fn0, bf(w_gu_dense[0]), bf(w_down_dense[0]), h1)
    gains = jnp.stack([kv_norm, attn_norm[1]])
    h3, ykv, yq = _ple_norm(h2, p2[0], bf(w_ple_proj[0]), bf(w_ple_gate[0]), gains)

    kv = _matmul(ykv, bf(w_kv))
    q1 = _matmul(yq, bf(w_q_b[0]), scale=inv_sqrt_d)
    o1 = _sb_attention(q1, kv, tri_ge, batch=batch, seq=seq, d_model=d)
    rw = jnp.pad(router_w[0], ((0, 0), (0, LANES - N_EXPERTS)))
    rb = jnp.pad(router_b[0], (0, LANES - N_EXPERTS)).reshape(1, LANES)
    h4, fn1, idx, gates = _oproj(o1, bf(w_o_b[0]), h3, ffn_norm[1:2], router=(rw, rb))

    tm = 256
    dest3, tile_expert, tile_valid, n_tiles = _routing_tables(idx[:, :TOP_K], t, tm=tm)
    xb = _dispatch(fn1, dest3, jnp.zeros((n_tiles * EXPERT_TILE, d), F32), tm=tm)
    yb = _experts(xb, bf(w_gu_moe[0]), bf(w_down_moe[0]), tile_expert, tile_valid)
    out = _combine(dest3, yb, h4, gates, p2[1], bf(w_ple_proj[1]), bf(w_ple_gate[1]),
                   final_norm.reshape(1, d), tm=tm)
    return out.reshape(batch, seq, d)
```

```python
import functools
import math

import jax
import jax.numpy as jnp
from jax import lax
from jax.experimental import pallas as pl
from jax.experimental.pallas import tpu as pltpu

HEAD_DIM = 64
LANES = 128
HEADS_PER_BLOCK = LANES // HEAD_DIM
N_EXPERTS = 8
TOP_K = 2
RMS_EPS = 1e-6
ATTN_TILE = 512
CUMSUM_CHUNK = 256
EXPERT_TILE = 512
VMEM_LIMIT = 56 * 1024 * 1024

BF16 = jnp.bfloat16
F32 = jnp.float32


def _params(*semantics):
    return pltpu.CompilerParams(dimension_semantics=semantics, vmem_limit_bytes=VMEM_LIMIT)


def _dot(a, b):
    return jnp.dot(a, b, preferred_element_type=F32)


def _dot_nt(a, b):
    return lax.dot_general(a, b, (((1,), (1,)), ((), ())), preferred_element_type=F32)


def _rstd(x):
    return lax.rsqrt(jnp.mean(x * x, axis=-1, keepdims=True) + RMS_EPS)


def _log_sigmoid(u):
    return jnp.minimum(u, 0.0) - jnp.log1p(jnp.exp(-jnp.abs(u)))


def _split3(x):
    hi = x.astype(BF16)
    r1 = x - hi.astype(F32)
    mid = r1.astype(BF16)
    lo = (r1 - mid.astype(F32)).astype(BF16)
    return hi, mid, lo


def _qkvf_kernel(x_ref, g_ref, w_ref, wf_ref, bf_ref, qkv_ref, lf_ref, xn_ref, *, n_q_blocks):
    j = pl.program_id(1)

    @pl.when(j == 0)
    def _():
        x = x_ref[...]
        xn = (x * _rstd(x) * g_ref[...]).astype(BF16)
        xn_ref[...] = xn
        f = _dot_nt(wf_ref[...], xn) + bf_ref[...]
        lf_ref[...] = _log_sigmoid(f)

    scale = jnp.where(j < n_q_blocks, 1.0 / math.sqrt(HEAD_DIM), 1.0).astype(F32)
    qkv_ref[...] = (_dot(xn_ref[...], w_ref[...]) * scale).astype(BF16)


def _qkvf_proj(x, g, w_qkv, w_f_t, b_f, *, tm=512, tn=1024):
    t, d = x.shape
    n = w_qkv.shape[1]
    h = w_f_t.shape[0]
    return pl.pallas_call(
        functools.partial(_qkvf_kernel, n_q_blocks=d // tn),
        grid=(t // tm, n // tn),
        in_specs=[
            pl.BlockSpec((tm, d), lambda i, j: (i, 0)),
            pl.BlockSpec((1, d), lambda i, j: (0, 0)),
            pl.BlockSpec((d, tn), lambda i, j: (0, j)),
            pl.BlockSpec((h, d), lambda i, j: (0, 0)),
            pl.BlockSpec((h, 1), lambda i, j: (0, 0)),
        ],
        out_specs=[
            pl.BlockSpec((tm, tn), lambda i, j: (i, j)),
            pl.BlockSpec((h, tm), lambda i, j: (0, i)),
        ],
        out_shape=[
            jax.ShapeDtypeStruct((t, n), BF16),
            jax.ShapeDtypeStruct((h, t), F32),
        ],
        scratch_shapes=[pltpu.VMEM((tm, d), BF16)],
        compiler_params=_params("parallel", "arbitrary"),
        name="qkvf_proj",
    )(x, g, w_qkv, w_f_t, b_f)


def _cumsum_kernel(lf_ref, tri_ref, c_ref):
    h, s = lf_ref.shape
    tri = tri_ref[...]
    carry = jnp.zeros((h, 1), F32)
    for c0 in range(0, s, CUMSUM_CHUNK):
        hi, mid, lo = _split3(lf_ref[:, c0:c0 + CUMSUM_CHUNK])
        cs = _dot(lo, tri) + _dot(mid, tri) + _dot(hi, tri) + carry
        for hd in range(h):
            c_ref[hd, :, c0:c0 + CUMSUM_CHUNK] = cs[hd:hd + 1, :]
        carry = cs[:, CUMSUM_CHUNK - 1:CUMSUM_CHUNK]


def _forget_cumsum(lf_t, tri, *, seq):
    h, t = lf_t.shape
    return pl.pallas_call(
        _cumsum_kernel,
        grid=(t // seq,),
        in_specs=[
            pl.BlockSpec((h, seq), lambda b: (0, b)),
            pl.BlockSpec((CUMSUM_CHUNK, CUMSUM_CHUNK), lambda b: (0, 0)),
        ],
        out_specs=pl.BlockSpec((h, 1, seq), lambda b: (0, 0, b)),
        out_shape=jax.ShapeDtypeStruct((h, 1, t), F32),
        compiler_params=_params("parallel"),
        name="forget_cumsum",
    )(lf_t, tri)


def _head_masks(shape):
    lane = lax.broadcasted_iota(jnp.int32, shape, 1)
    return lane < HEAD_DIM


def _split_heads(x2, first_head_lanes, fill):
    return (jnp.where(first_head_lanes, x2, fill), jnp.where(first_head_lanes, fill, x2))


def _lane_tile(x, width):
    return jnp.concatenate([x] * (width // LANES), axis=1)


def _fox_kernel(q_ref, k_ref, v_ref, c_ref, o_ref, m_ref, acc_ref):
    qi = pl.program_id(2)
    tq = q_ref.shape[0]
    q2 = q_ref[...]
    first = _head_masks(q2.shape)
    qs = jnp.concatenate(_split_heads(q2, first, jnp.zeros_like(q2)), axis=0)
    m_ref[...] = jnp.full(m_ref.shape, -jnp.inf, F32)
    acc_ref[...] = jnp.zeros(acc_ref.shape, F32)

    def tile(kt, diagonal):
        k0 = pl.multiple_of(kt * tq, tq)
        k = k_ref[pl.ds(k0, tq), :]
        v = v_ref[pl.ds(k0, tq), :]
        v_one = jnp.concatenate([v, jnp.ones_like(v)], axis=1)
        s = _dot_nt(qs, k)
        s = jnp.concatenate([s[hd * tq:(hd + 1) * tq] - c_ref[hd, :, pl.ds(k0, tq)]
                             for hd in range(HEADS_PER_BLOCK)], axis=0)
        if diagonal:
            row = lax.broadcasted_iota(jnp.int32, s.shape, 0) & (tq - 1)
            col = lax.broadcasted_iota(jnp.int32, s.shape, 1)
            s = jnp.where(col <= row, s, -jnp.inf)
        m_prev = m_ref[...]
        m_new = jnp.maximum(m_prev, jnp.max(s, axis=1, keepdims=True))
        alpha = jnp.exp(m_prev - m_new)
        p = jnp.exp(s - _lane_tile(m_new, tq))
        acc_ref[...] = _lane_tile(alpha, 2 * LANES) * acc_ref[...] + _dot(p.astype(BF16), v_one)
        m_ref[...] = m_new

    def body(kt, carry):
        tile(kt, False)
        return carry

    lax.fori_loop(0, qi, body, 0)
    tile(qi, True)

    out = acc_ref[:, :LANES] / acc_ref[:, LANES:]
    o_ref[...] = jnp.where(first, out[:tq], out[tq:]).astype(BF16)


def _fox_attention(qkv, c, *, batch, seq, d_model):
    t = qkv.shape[0]
    n_pairs = d_model // LANES
    tq = ATTN_TILE
    nq = seq // tq
    assert tq & (tq - 1) == 0, "the diagonal mask takes row % tq as row & (tq - 1)"
    rows = HEADS_PER_BLOCK * tq
    return pl.pallas_call(
        _fox_kernel,
        grid=(batch, n_pairs, nq),
        in_specs=[
            pl.BlockSpec((tq, LANES), lambda b, hp, qi: (b * nq + qi, hp)),
            pl.BlockSpec((seq, LANES), lambda b, hp, qi: (b, n_pairs + hp)),
            pl.BlockSpec((seq, LANES), lambda b, hp, qi: (b, 2 * n_pairs + hp)),
            pl.BlockSpec((HEADS_PER_BLOCK, 1, seq), lambda b, hp, qi: (hp, 0, b)),
        ],
        out_specs=pl.BlockSpec((tq, LANES), lambda b, hp, qi: (b * nq + qi, hp)),
        out_shape=jax.ShapeDtypeStruct((t, d_model), BF16),
        scratch_shapes=[pltpu.VMEM((rows, LANES), F32), pltpu.VMEM((rows, 2 * LANES), F32)],
        compiler_params=_params("parallel", "parallel", "arbitrary"),
        name="fox_attention",
    )(qkv, qkv, qkv, c)


def _sb_kernel(q_ref, k_ref, v_ref, tri_ref, o_ref, r_ref, acc_ref):
    qi = pl.program_id(2)
    tq = q_ref.shape[0]
    sub = CUMSUM_CHUNK
    n_sub = tq // sub
    q2 = q_ref[...]
    first = _head_masks(q2.shape)
    qs = jnp.concatenate(_split_heads(q2, first, jnp.zeros_like(q2)), axis=0)
    r_ref[...] = jnp.zeros(r_ref.shape, F32)
    acc_ref[...] = jnp.zeros(acc_ref.shape, F32)

    def tile(kt, diagonal):
        k0 = pl.multiple_of(kt * tq, tq)
        k = k_ref[pl.ds(k0, tq), :]
        v = v_ref[pl.ds(k0, tq), :]
        z = _dot_nt(qs, k)
        lm = jnp.minimum(-z, 0.0) - jnp.log(1.0 + jnp.exp(-jnp.abs(z)))
        if diagonal:
            row = lax.broadcasted_iota(jnp.int32, z.shape, 0) & (tq - 1)
            col = lax.broadcasted_iota(jnp.int32, z.shape, 1)
            visible = col < row
            lm = jnp.where(visible, lm, 0.0)
        hi = lm.astype(BF16)
        lo = (lm - hi.astype(F32)).astype(BF16)
        p = [_dot(jnp.concatenate([hi[:, sb * sub:(sb + 1) * sub], lo[:, sb * sub:(sb + 1) * sub]], axis=1),
                  tri_ref[...]) for sb in range(n_sub)]
        r = r_ref[...]
        later = [None] * n_sub
        for sb in reversed(range(n_sub)):
            later[sb] = p[sb] + _lane_tile(r, sub)
            r = r + p[sb][:, 0:1]
        w = jnp.exp(z + jnp.concatenate(later, axis=1))
        if diagonal:
            w = jnp.where(visible, w, 0.0)
        acc_ref[...] += _dot(w.astype(BF16), v)
        r_ref[...] = r

    tile(qi, True)

    def body(i, carry):
        tile(qi - 1 - i, False)
        return carry

    lax.fori_loop(0, qi, body, 0)
    o_ref[...] = jnp.where(first, acc_ref[:tq], acc_ref[tq:]).astype(BF16)


def _sb_attention(q, kv, tri2, *, batch, seq, d_model):
    t = q.shape[0]
    n_pairs = d_model // LANES
    tq = ATTN_TILE
    nq = seq // tq
    assert tq & (tq - 1) == 0, "the diagonal mask takes row % tq as row & (tq - 1)"
    stat = pltpu.VMEM((HEADS_PER_BLOCK * tq, LANES), F32)
    return pl.pallas_call(
        _sb_kernel,
        grid=(batch, n_pairs, nq),
        in_specs=[
            pl.BlockSpec((tq, LANES), lambda b, hp, qi: (b * nq + qi, hp)),
            pl.BlockSpec((seq, LANES), lambda b, hp, qi: (b, hp)),
            pl.BlockSpec((seq, LANES), lambda b, hp, qi: (b, n_pairs + hp)),
            pl.BlockSpec(tri2.shape, lambda b, hp, qi: (0, 0)),
        ],
        out_specs=pl.BlockSpec((tq, LANES), lambda b, hp, qi: (b * nq + qi, hp)),
        out_shape=jax.ShapeDtypeStruct((t, d_model), BF16),
        scratch_shapes=[stat, stat],
        compiler_params=_params("parallel", "parallel", "arbitrary"),
        name="sb_attention",
    )(q, kv, kv, tri2)


def _oproj_kernel(o_ref, w_ref, h_ref, g_ref, h1_ref, fn_ref):
    h1 = h_ref[...] + _dot(o_ref[...], w_ref[...])
    h1_ref[...] = h1
    fn_ref[...] = (h1 * _rstd(h1) * g_ref[...]).astype(fn_ref.dtype)


def _oproj_router_kernel(o_ref, w_ref, h_ref, g_ref, rw_ref, rb_ref, h1_ref, fn_ref, idx_ref, gate_ref):
    h1 = h_ref[...] + _dot(o_ref[...], w_ref[...])
    h1_ref[...] = h1
    fn = h1 * _rstd(h1) * g_ref[...]
    fn_ref[...] = fn
    logits = jnp.dot(fn, rw_ref[...], preferred_element_type=F32,
                     precision=lax.Precision.HIGHEST) + rb_ref[...]
    lane = lax.broadcasted_iota(jnp.int32, logits.shape, 1)
    logits = jnp.where(lane < N_EXPERTS, logits, -jnp.inf)
    lane_f = lane.astype(F32)
    m1 = jnp.max(logits, axis=1, keepdims=True)
    i1 = jnp.min(jnp.where(logits == m1, lane_f, float(LANES)), axis=1, keepdims=True)
    rest = jnp.where(lane_f == i1, -jnp.inf, logits)
    m2 = jnp.max(rest, axis=1, keepdims=True)
    i2 = jnp.min(jnp.where(rest == m2, lane_f, float(LANES)), axis=1, keepdims=True)
    e2 = jnp.exp(m2 - m1)
    g1 = 1.0 / (1.0 + e2)
    g2 = e2 / (1.0 + e2)
    idx_ref[...] = jnp.where(lane == 0, i1, jnp.where(lane == 1, i2, 0.0)).astype(jnp.int32)
    gate_ref[...] = jnp.where(lane == 0, g1, jnp.where(lane == 1, g2, 0.0))


def _oproj(o, w_o, h, g, router=None, *, tm=512):
    t, d = h.shape
    row = lambda i: (i, 0)
    const = lambda i: (0, 0)
    in_specs = [
        pl.BlockSpec((tm, d), row),
        pl.BlockSpec((d, d), const),
        pl.BlockSpec((tm, d), row),
        pl.BlockSpec((1, d), const),
    ]
    if router is None:
        return pl.pallas_call(
            _oproj_kernel,
            grid=(t // tm,),
            in_specs=in_specs,
            out_specs=[pl.BlockSpec((tm, d), row), pl.BlockSpec((tm, d), row)],
            out_shape=[jax.ShapeDtypeStruct((t, d), F32), jax.ShapeDtypeStruct((t, d), BF16)],
            compiler_params=_params("parallel"),
            name="oproj_norm",
        )(o, w_o, h, g)
    rw, rb = router
    return pl.pallas_call(
        _oproj_router_kernel,
        grid=(t // tm,),
        in_specs=in_specs + [pl.BlockSpec((d, LANES), const), pl.BlockSpec((1, LANES), const)],
        out_specs=[pl.BlockSpec((tm, d), row), pl.BlockSpec((tm, d), row),
                   pl.BlockSpec((tm, LANES), row), pl.BlockSpec((tm, LANES), row)],
        out_shape=[jax.ShapeDtypeStruct((t, d), F32), jax.ShapeDtypeStruct((t, d), F32),
                   jax.ShapeDtypeStruct((t, LANES), jnp.int32), jax.ShapeDtypeStruct((t, LANES), F32)],
        compiler_params=_params("parallel"),
        name="oproj_norm_router",
    )(o, w_o, h, g, rw, rb)


def _swiglu_chunk(x, wg, wu, wd):
    g = _dot(x, wg)
    u = _dot(x, wu)
    act = (g * (1.0 / (1.0 + jnp.exp(-g))) * u).astype(BF16)
    return _dot(act, wd)


def _dense_ffn_kernel(x_ref, wg_ref, wu_ref, wd_ref, h_ref, o_ref):
    @pl.when(pl.program_id(1) == 0)
    def _():
        o_ref[...] = h_ref[...]

    o_ref[...] += _swiglu_chunk(x_ref[...], wg_ref[...], wu_ref[...], wd_ref[...])


def _dense_ffn(xn, w_gu, w_down, h, *, tm=1024, tf=256):
    t, d = h.shape
    f = w_down.shape[0]
    nf = f // tf
    return pl.pallas_call(
        _dense_ffn_kernel,
        grid=(t // tm, nf),
        in_specs=[
            pl.BlockSpec((tm, d), lambda i, j: (i, 0)),
            pl.BlockSpec((d, tf), lambda i, j: (0, j)),
            pl.BlockSpec((d, tf), lambda i, j: (0, nf + j)),
            pl.BlockSpec((tf, d), lambda i, j: (j, 0)),
            pl.BlockSpec((tm, d), lambda i, j: (i, 0)),
        ],
        out_specs=pl.BlockSpec((tm, d), lambda i, j: (i, 0)),
        out_shape=jax.ShapeDtypeStruct((t, d), F32),
        compiler_params=_params("parallel", "arbitrary"),
        name="dense_ffn",
    )(xn, w_gu, w_gu, w_down, h)


def _ple(h2, p, wp, wg):
    gate = 1.0 / (1.0 + jnp.exp(-_dot(h2.astype(BF16), wg)))
    return h2 + _dot(p.astype(BF16), wp) * gate


def _ple_norm_kernel(h_ref, p_ref, wp_ref, wg_ref, gains_ref, h3_ref, *y_refs):
    h3 = _ple(h_ref[...], p_ref[...], wp_ref[...], wg_ref[...])
    h3_ref[...] = h3
    y = h3 * _rstd(h3)
    for n, y_ref in enumerate(y_refs):
        y_ref[...] = (y * gains_ref[n:n + 1, :]).astype(y_ref.dtype)


def _ple_norm(h2, p, wp, wg, gains, *, tm=512):
    t, d = h2.shape
    dp = p.shape[1]
    n_g = gains.shape[0]
    row = lambda i: (i, 0)
    const = lambda i: (0, 0)
    return pl.pallas_call(
        _ple_norm_kernel,
        grid=(t // tm,),
        in_specs=[
            pl.BlockSpec((tm, d), row),
            pl.BlockSpec((tm, dp), row),
            pl.BlockSpec((dp, d), const),
            pl.BlockSpec((d, d), const),
            pl.BlockSpec((n_g, d), const),
        ],
        out_specs=[pl.BlockSpec((tm, d), row)] * (1 + n_g),
        out_shape=[jax.ShapeDtypeStruct((t, d), F32)] + [jax.ShapeDtypeStruct((t, d), BF16)] * n_g,
        compiler_params=_params("parallel"),
        name="ple_norm",
    )(h2, p, wp, wg, gains)


def _matmul_kernel(x_ref, w_ref, o_ref, *, scale):
    o_ref[...] = (_dot(x_ref[...], w_ref[...]) * scale).astype(o_ref.dtype)


def _matmul(x, w, *, scale=1.0, tm=512, tn=1024):
    t, d = x.shape
    n = w.shape[1]
    return pl.pallas_call(
        functools.partial(_matmul_kernel, scale=scale),
        grid=(t // tm, n // tn),
        in_specs=[pl.BlockSpec((tm, d), lambda i, j: (i, 0)), pl.BlockSpec((d, tn), lambda i, j: (0, j))],
        out_specs=pl.BlockSpec((tm, tn), lambda i, j: (i, j)),
        out_shape=jax.ShapeDtypeStruct((t, n), BF16),
        compiler_params=_params("parallel", "parallel"),
        name="matmul",
    )(x, w)


def _dispatch_kernel(dest_ref, x_ref, xb_in_hbm, xb_hbm, sem, *, tm):
    del xb_in_hbm

    def issue(r, carry):
        for k in range(TOP_K):
            pltpu.make_async_copy(x_ref.at[pl.ds(r, 1)],
                                  xb_hbm.at[pl.ds(dest_ref[0, 0, r * TOP_K + k], 1)], sem).start()
        return carry

    lax.fori_loop(0, tm, issue, 0, unroll=8)
    for _ in range(TOP_K):
        pltpu.make_async_copy(x_ref, xb_hbm.at[pl.ds(0, tm)], sem).wait()


def _dispatch(x, dest3, xb_zero, *, tm):
    t, d = x.shape
    return pl.pallas_call(
        functools.partial(_dispatch_kernel, tm=tm),
        grid=(t // tm,),
        in_specs=[
            pl.BlockSpec((1, 1, tm * TOP_K), lambda i: (i, 0, 0), memory_space=pltpu.SMEM),
            pl.BlockSpec((tm, d), lambda i: (i, 0)),
            pl.BlockSpec(memory_space=pl.ANY),
        ],
        out_specs=pl.BlockSpec(memory_space=pl.ANY),
        out_shape=jax.ShapeDtypeStruct(xb_zero.shape, xb_zero.dtype),
        scratch_shapes=[pltpu.SemaphoreType.DMA],
        input_output_aliases={2: 0},
        compiler_params=_params("arbitrary"),
        name="moe_dispatch",
    )(dest3, x, xb_zero)


def _expert_kernel(te_ref, tv_ref, x_ref, wg_ref, wu_ref, wd_ref, y_ref, xs_ref):
    ti = pl.program_id(0)
    j = pl.program_id(1)

    @pl.when(j == 0)
    def _():
        xs_ref[...] = x_ref[...].astype(BF16)
        y_ref[...] = jnp.zeros(y_ref.shape, F32)

    @pl.when(tv_ref[ti] > 0)
    def _():
        y_ref[...] += _swiglu_chunk(xs_ref[...], wg_ref[0], wu_ref[0], wd_ref[0])


def _experts(xb, w_gu, w_down, tile_expert, tile_valid, *, fc=512):
    n_rows, d = xb.shape
    fe = w_down.shape[1]
    nj = fe // fc
    n_tiles = n_rows // EXPERT_TILE

    def jj(ti, j, tv):
        return jnp.where(tv[ti] > 0, j, nj - 1)

    return pl.pallas_call(
        _expert_kernel,
        grid_spec=pltpu.PrefetchScalarGridSpec(
            num_scalar_prefetch=2,
            grid=(n_tiles, nj),
            in_specs=[
                pl.BlockSpec((EXPERT_TILE, d), lambda ti, j, te, tv: (ti, 0)),
                pl.BlockSpec((1, d, fc), lambda ti, j, te, tv: (te[ti], 0, jj(ti, j, tv))),
                pl.BlockSpec((1, d, fc), lambda ti, j, te, tv: (te[ti], 0, nj + jj(ti, j, tv))),
                pl.BlockSpec((1, fc, d), lambda ti, j, te, tv: (te[ti], jj(ti, j, tv), 0)),
            ],
            out_specs=pl.BlockSpec((EXPERT_TILE, d), lambda ti, j, te, tv: (ti, 0)),
            scratch_shapes=[pltpu.VMEM((EXPERT_TILE, d), BF16)],
        ),
        out_shape=jax.ShapeDtypeStruct((n_rows, d), F32),
        compiler_params=_params("parallel", "arbitrary"),
        name="moe_experts",
    )(tile_expert, tile_valid, xb, w_gu, w_gu, w_down)


def _combine_kernel(dest_ref, yb_hbm, h_ref, gate_ref, p_ref, wp_ref, wg_ref, g_ref, o_ref, y_ref, sem, *, tm):
    def issue(r, carry):
        for k in range(TOP_K):
            pltpu.make_async_copy(yb_hbm.at[pl.ds(dest_ref[0, 0, r * TOP_K + k], 1)],
                                  y_ref.at[k, pl.ds(r, 1)], sem).start()
        return carry

    lax.fori_loop(0, tm, issue, 0, unroll=8)
    for k in range(TOP_K):
        pltpu.make_async_copy(yb_hbm.at[pl.ds(0, tm)], y_ref.at[k], sem).wait()

    gates = gate_ref[...]
    h2 = h_ref[...] + (y_ref[0] * gates[:, 0:1] + y_ref[1] * gates[:, 1:2])
    h3 = _ple(h2, p_ref[...], wp_ref[...], wg_ref[...])
    o_ref[...] = h3 * _rstd(h3) * g_ref[...]


def _combine(dest3, yb, h, gates, p, wp, wg, g, *, tm):
    t, d = h.shape
    dp = p.shape[1]
    row = lambda i: (i, 0)
    const = lambda i: (0, 0)
    return pl.pallas_call(
        functools.partial(_combine_kernel, tm=tm),
        grid=(t // tm,),
        in_specs=[
            pl.BlockSpec((1, 1, tm * TOP_K), lambda i: (i, 0, 0), memory_space=pltpu.SMEM),
            pl.BlockSpec(memory_space=pl.ANY),
            pl.BlockSpec((tm, d), row),
            pl.BlockSpec((tm, LANES), row),
            pl.BlockSpec((tm, dp), row),
            pl.BlockSpec((dp, d), const),
            pl.BlockSpec((d, d), const),
            pl.BlockSpec((1, d), const),
        ],
        out_specs=pl.BlockSpec((tm, d), row),
        out_shape=jax.ShapeDtypeStruct((t, d), F32),
        scratch_shapes=[pltpu.VMEM((TOP_K, tm, d), F32), pltpu.SemaphoreType.DMA],
        compiler_params=_params("arbitrary"),
        name="moe_combine",
    )(dest3, yb, h, gates, p, wp, wg, g)


def _routing_tables(idx, n_tokens, *, tm):
    e_flat = idx.reshape(-1)
    onehot = (e_flat[:, None] == jnp.arange(N_EXPERTS, dtype=jnp.int32)[None, :]).astype(jnp.int32)
    csum = jnp.cumsum(onehot, axis=0)
    counts = csum[-1]
    padded = (counts + EXPERT_TILE - 1) // EXPERT_TILE * EXPERT_TILE
    pend = jnp.cumsum(padded)
    pstart = pend - padded
    dest = jnp.sum(onehot * (csum - 1 + pstart[None, :]), axis=1)
    n_tiles = (n_tokens * TOP_K) // EXPERT_TILE + N_EXPERTS
    tile_row0 = jnp.arange(n_tiles, dtype=jnp.int32) * EXPERT_TILE
    tile_expert = jnp.minimum(jnp.sum((tile_row0[:, None] >= pend[None, :]).astype(jnp.int32), axis=1),
                              N_EXPERTS - 1)
    tile_valid = (tile_row0 < pend[-1]).astype(jnp.int32)
    dest3 = dest.astype(jnp.int32).reshape(n_tokens // tm, 1, tm * TOP_K)
    return dest3, tile_expert, tile_valid, n_tiles


def kernel(x, p, attn_norm, ffn_norm, w_in_a, b_f, w_o_a, kv_norm, w_kv, w_q_b, w_o_b, w_gu_dense, w_down_dense, router_w, router_b, w_gu_moe, w_down_moe, w_ple_proj, w_ple_gate, final_norm):
    batch, seq, d = x.shape
    t = batch * seq
    n_heads = d // HEAD_DIM
    inv_sqrt_d = 1.0 / math.sqrt(HEAD_DIM)
    x2 = x.reshape(t, d)
    p2 = p.reshape(p.shape[0], t, p.shape[-1])
    bf = lambda w: w.astype(BF16)

    ii = lax.broadcasted_iota(jnp.int32, (CUMSUM_CHUNK, CUMSUM_CHUNK), 0)
    jj = lax.broadcasted_iota(jnp.int32, (CUMSUM_CHUNK, CUMSUM_CHUNK), 1)
    tri_le = (ii <= jj).astype(BF16)
    tri_ge = (ii >= jj).astype(BF16)
    tri_ge2 = jnp.concatenate([tri_ge, tri_ge], axis=0)

    w_in = w_in_a[0]
    qkv, lf_t = _qkvf_proj(x2, attn_norm[0:1], bf(w_in[:, :3 * d]), bf(w_in[:, 3 * d:].T),
                           b_f[0].reshape(n_heads, 1))
    c = _forget_cumsum(lf_t, tri_le, seq=seq)
    o0 = _fox_attention(qkv, c, batch=batch, seq=seq, d_model=d)
    h1, fn0 = _oproj(o0, bf(w_o_a[0]), x2, ffn_norm[0:1])
    h2 = _dense_ffn(fn0, bf(w_gu_dense[0]), bf(w_down_dense[0]), h1)
    gains = jnp.stack([kv_norm, attn_norm[1]])
    h3, ykv, yq = _ple_norm(h2, p2[0], bf(w_ple_proj[0]), bf(w_ple_gate[0]), gains)

    kv = _matmul(ykv, bf(w_kv))
    q1 = _matmul(yq, bf(w_q_b[0]), scale=inv_sqrt_d)
    o1 = _sb_attention(q1, kv, tri_ge2, batch=batch, seq=seq, d_model=d)
    rw = jnp.pad(router_w[0], ((0, 0), (0, LANES - N_EXPERTS)))
    rb = jnp.pad(router_b[0], (0, LANES - N_EXPERTS)).reshape(1, LANES)
    h4, fn1, idx, gates = _oproj(o1, bf(w_o_b[0]), h3, ffn_norm[1:2], router=(rw, rb))

    tm = 512
    dest3, tile_expert, tile_valid, n_tiles = _routing_tables(idx[:, :TOP_K], t, tm=tm)
    xb = _dispatch(fn1, dest3, jnp.zeros((n_tiles * EXPERT_TILE, d), F32), tm=tm)
    yb = _experts(xb, bf(w_gu_moe[0]), bf(w_down_moe[0]), tile_expert, tile_valid)
    out = _combine(dest3, yb, h4, gates, p2[1], bf(w_ple_proj[1]), bf(w_ple_gate[1]),
                   final_norm.reshape(1, d), tm=tm)
    return out.reshape(batch, seq, d)
```

```python
import functools
import math

import jax
import jax.numpy as jnp
from jax import lax
from jax.experimental import pallas as pl
from jax.experimental.pallas import tpu as pltpu

HEAD_DIM = 64
LANES = 128
HEADS_PER_BLOCK = LANES // HEAD_DIM
N_EXPERTS = 8
TOP_K = 2
RMS_EPS = 1e-6
ATTN_TILE = 512
CUMSUM_CHUNK = 256
EXPERT_TILE = 512
DENSE_CHUNK = 256
EXPERT_CHUNK = 512
VMEM_LIMIT = 56 * 1024 * 1024

BF16 = jnp.bfloat16
F32 = jnp.float32


def _params(*semantics):
    return pltpu.CompilerParams(dimension_semantics=semantics, vmem_limit_bytes=VMEM_LIMIT)


def _dot(a, b):
    return jnp.dot(a, b, preferred_element_type=F32)


def _dot_nt(a, b):
    return lax.dot_general(a, b, (((1,), (1,)), ((), ())), preferred_element_type=F32)


def _rstd(x):
    return lax.rsqrt(jnp.mean(x * x, axis=-1, keepdims=True) + RMS_EPS)


def _log_sigmoid(u):
    return jnp.minimum(u, 0.0) - jnp.log1p(jnp.exp(-jnp.abs(u)))


def _split3(x):
    hi = x.astype(BF16)
    r1 = x - hi.astype(F32)
    mid = r1.astype(BF16)
    lo = (r1 - mid.astype(F32)).astype(BF16)
    return hi, mid, lo


def _resident(block_shape, index_map):
    return pl.BlockSpec(block_shape, index_map, pipeline_mode=pl.Buffered(1))


def _qkvf_kernel(x_ref, g_ref, w_ref, wf_ref, bf_ref, qkv_ref, lf_ref, xn_ref, *, n_q_blocks):
    j = pl.program_id(1)

    @pl.when(j == 0)
    def _():
        x = x_ref[...]
        xn = (x * _rstd(x) * g_ref[...]).astype(BF16)
        xn_ref[...] = xn
        f = _dot_nt(wf_ref[...], xn) + bf_ref[...]
        lf_ref[...] = _log_sigmoid(f)

    scale = jnp.where(j < n_q_blocks, 1.0 / math.sqrt(HEAD_DIM), 1.0).astype(F32)
    qkv_ref[...] = (_dot(xn_ref[...], w_ref[...]) * scale).astype(BF16)


def _qkvf_proj(x, g, w_qkv, w_f_t, b_f, *, tm=512, tn=1024):
    t, d = x.shape
    n = w_qkv.shape[1]
    h = w_f_t.shape[0]
    return pl.pallas_call(
        functools.partial(_qkvf_kernel, n_q_blocks=d // tn),
        grid=(t // tm, n // tn),
        in_specs=[
            pl.BlockSpec((tm, d), lambda i, j: (i, 0)),
            pl.BlockSpec((1, d), lambda i, j: (0, 0)),
            pl.BlockSpec((d, tn), lambda i, j: (0, j)),
            pl.BlockSpec((h, d), lambda i, j: (0, 0)),
            pl.BlockSpec((h, 1), lambda i, j: (0, 0)),
        ],
        out_specs=[
            pl.BlockSpec((tm, tn), lambda i, j: (i, j)),
            pl.BlockSpec((h, tm), lambda i, j: (0, i)),
        ],
        out_shape=[
            jax.ShapeDtypeStruct((t, n), BF16),
            jax.ShapeDtypeStruct((h, t), F32),
        ],
        scratch_shapes=[pltpu.VMEM((tm, d), BF16)],
        compiler_params=_params("parallel", "arbitrary"),
        name="qkvf_proj",
    )(x, g, w_qkv, w_f_t, b_f)


def _cumsum_kernel(lf_ref, tri_ref, c_ref):
    h, s = lf_ref.shape
    tri = tri_ref[...]
    carry = jnp.zeros((h, 1), F32)
    for c0 in range(0, s, CUMSUM_CHUNK):
        hi, mid, lo = _split3(lf_ref[:, c0:c0 + CUMSUM_CHUNK])
        cs = _dot(lo, tri) + _dot(mid, tri) + _dot(hi, tri) + carry
        for hd in range(h):
            c_ref[hd, :, c0:c0 + CUMSUM_CHUNK] = cs[hd:hd + 1, :]
        carry = cs[:, CUMSUM_CHUNK - 1:CUMSUM_CHUNK]


def _forget_cumsum(lf_t, tri, *, seq):
    h, t = lf_t.shape
    return pl.pallas_call(
        _cumsum_kernel,
        grid=(t // seq,),
        in_specs=[
            pl.BlockSpec((h, seq), lambda b: (0, b)),
            pl.BlockSpec((CUMSUM_CHUNK, CUMSUM_CHUNK), lambda b: (0, 0)),
        ],
        out_specs=pl.BlockSpec((h, 1, seq), lambda b: (0, 0, b)),
        out_shape=jax.ShapeDtypeStruct((h, 1, t), F32),
        compiler_params=_params("parallel"),
        name="forget_cumsum",
    )(lf_t, tri)


def _head_masks(shape):
    lane = lax.broadcasted_iota(jnp.int32, shape, 1)
    return lane < HEAD_DIM


def _split_heads(x2, first_head_lanes, fill):
    return (jnp.where(first_head_lanes, x2, fill), jnp.where(first_head_lanes, fill, x2))


def _stack_heads(q2, first_head_lanes):
    return jnp.concatenate(_split_heads(q2, first_head_lanes, jnp.zeros_like(q2)), axis=0)


def _lane_tile(x, width):
    return jnp.concatenate([x] * (width // LANES), axis=1)


def _fox_kernel(q_ref, k_ref, v_ref, c_ref, o_ref, *scratch):
    tq = ATTN_TILE
    nq = q_ref.shape[0] // tq
    m_refs, acc_refs = scratch[:nq], scratch[nq:]
    first = _head_masks((tq, LANES))

    def tile(qs, m_ref, acc_ref, kt, diagonal):
        k = k_ref[kt * tq:(kt + 1) * tq, :]
        v = v_ref[kt * tq:(kt + 1) * tq, :]
        v_one = jnp.concatenate([v, jnp.ones_like(v)], axis=1)
        s = _dot_nt(qs, k)
        s = jnp.concatenate([s[hd * tq:(hd + 1) * tq] - c_ref[hd, :, kt * tq:(kt + 1) * tq]
                             for hd in range(HEADS_PER_BLOCK)], axis=0)
        if diagonal:
            row = lax.broadcasted_iota(jnp.int32, s.shape, 0) & (tq - 1)
            col = lax.broadcasted_iota(jnp.int32, s.shape, 1)
            s = jnp.where(col <= row, s, -jnp.inf)
        m_prev = m_ref[...]
        m_new = jnp.maximum(m_prev, jnp.max(s, axis=1, keepdims=True))
        alpha = jnp.exp(m_prev - m_new)
        p = jnp.exp(s - _lane_tile(m_new, tq))
        acc_ref[...] = _lane_tile(alpha, 2 * LANES) * acc_ref[...] + _dot(p.astype(BF16), v_one)
        m_ref[...] = m_new

    for qt in range(nq):
        qs = _stack_heads(q_ref[qt * tq:(qt + 1) * tq, :], first)
        m_ref, acc_ref = m_refs[qt], acc_refs[qt]
        m_ref[...] = jnp.full(m_ref.shape, -jnp.inf, F32)
        acc_ref[...] = jnp.zeros(acc_ref.shape, F32)
        for kt in range(qt + 1):
            tile(qs, m_ref, acc_ref, kt, kt == qt)
        out = acc_ref[:, :LANES] / acc_ref[:, LANES:]
        o_ref[qt * tq:(qt + 1) * tq, :] = jnp.where(first, out[:tq], out[tq:]).astype(BF16)


def _fox_attention(qkv, c, *, batch, seq, d_model):
    t = qkv.shape[0]
    n_pairs = d_model // LANES
    tq = ATTN_TILE
    nq = seq // tq
    assert tq & (tq - 1) == 0, "the diagonal mask takes row % tq as row & (tq - 1)"
    rows = HEADS_PER_BLOCK * tq
    return pl.pallas_call(
        _fox_kernel,
        grid=(batch, n_pairs),
        in_specs=[
            pl.BlockSpec((seq, LANES), lambda b, hp: (b, hp)),
            pl.BlockSpec((seq, LANES), lambda b, hp: (b, n_pairs + hp)),
            pl.BlockSpec((seq, LANES), lambda b, hp: (b, 2 * n_pairs + hp)),
            pl.BlockSpec((HEADS_PER_BLOCK, 1, seq), lambda b, hp: (hp, 0, b)),
        ],
        out_specs=pl.BlockSpec((seq, LANES), lambda b, hp: (b, hp)),
        out_shape=jax.ShapeDtypeStruct((t, d_model), BF16),
        scratch_shapes=([pltpu.VMEM((rows, LANES), F32)] * nq + [pltpu.VMEM((rows, 2 * LANES), F32)] * nq),
        compiler_params=_params("parallel", "parallel"),
        name="fox_attention",
    )(qkv, qkv, qkv, c)


def _sb_kernel(q_ref, k_ref, v_ref, tri_ref, o_ref, *scratch):
    tq = ATTN_TILE
    nq = q_ref.shape[0] // tq
    r_refs, acc_refs = scratch[:nq], scratch[nq:]
    sub = CUMSUM_CHUNK
    n_sub = tq // sub
    first = _head_masks((tq, LANES))

    def tile(qs, r_ref, acc_ref, kt, diagonal):
        k = k_ref[kt * tq:(kt + 1) * tq, :]
        v = v_ref[kt * tq:(kt + 1) * tq, :]
        z = _dot_nt(qs, k)
        nlm = jnp.maximum(z, 0.0) + jnp.log(1.0 + jnp.exp(-jnp.abs(z)))
        if diagonal:
            row = lax.broadcasted_iota(jnp.int32, z.shape, 0) & (tq - 1)
            col = lax.broadcasted_iota(jnp.int32, z.shape, 1)
            visible = col < row
            nlm = jnp.where(visible, nlm, 0.0)
        hi = nlm.astype(BF16)
        lo = (nlm - hi.astype(F32)).astype(BF16)
        p = [_dot(jnp.concatenate([hi[:, sb * sub:(sb + 1) * sub], lo[:, sb * sub:(sb + 1) * sub]], axis=1),
                  tri_ref[...]) for sb in range(n_sub)]
        r = r_ref[...]
        later = [None] * n_sub
        for sb in reversed(range(n_sub)):
            later[sb] = p[sb] + _lane_tile(r, sub)
            r = r + p[sb][:, 0:1]
        w = jnp.exp(z - jnp.concatenate(later, axis=1))
        if diagonal:
            w = jnp.where(visible, w, 0.0)
        acc_ref[...] += _dot(w.astype(BF16), v)
        r_ref[...] = r

    for qt in range(nq):
        qs = _stack_heads(q_ref[qt * tq:(qt + 1) * tq, :], first)
        r_ref, acc_ref = r_refs[qt], acc_refs[qt]
        r_ref[...] = jnp.zeros(r_ref.shape, F32)
        acc_ref[...] = jnp.zeros(acc_ref.shape, F32)
        for kt in reversed(range(qt + 1)):
            tile(qs, r_ref, acc_ref, kt, kt == qt)
        o_ref[qt * tq:(qt + 1) * tq, :] = jnp.where(first, acc_ref[:tq], acc_ref[tq:]).astype(BF16)


def _sb_attention(q, kv, tri2, *, batch, seq, d_model):
    t = q.shape[0]
    n_pairs = d_model // LANES
    tq = ATTN_TILE
    nq = seq // tq
    assert tq & (tq - 1) == 0, "the diagonal mask takes row % tq as row & (tq - 1)"
    stat = pltpu.VMEM((HEADS_PER_BLOCK * tq, LANES), F32)
    return pl.pallas_call(
        _sb_kernel,
        grid=(batch, n_pairs),
        in_specs=[
            pl.BlockSpec((seq, LANES), lambda b, hp: (b, hp)),
            pl.BlockSpec((seq, LANES), lambda b, hp: (b, hp)),
            pl.BlockSpec((seq, LANES), lambda b, hp: (b, n_pairs + hp)),
            pl.BlockSpec(tri2.shape, lambda b, hp: (0, 0)),
        ],
        out_specs=pl.BlockSpec((seq, LANES), lambda b, hp: (b, hp)),
        out_shape=jax.ShapeDtypeStruct((t, d_model), BF16),
        scratch_shapes=[stat] * (2 * nq),
        compiler_params=_params("parallel", "parallel"),
        name="sb_attention",
    )(q, kv, kv, tri2)


def _oproj_kernel(o_ref, w_ref, h_ref, g_ref, h1_ref, fn_ref):
    h1 = h_ref[...] + _dot(o_ref[...], w_ref[...])
    h1_ref[...] = h1
    fn_ref[...] = (h1 * _rstd(h1) * g_ref[...]).astype(fn_ref.dtype)


def _oproj_router_kernel(o_ref, w_ref, h_ref, g_ref, rw_ref, rb_ref, h1_ref, fn_ref, idx_ref, gate_ref):
    h1 = h_ref[...] + _dot(o_ref[...], w_ref[...])
    h1_ref[...] = h1
    fn = h1 * _rstd(h1) * g_ref[...]
    fn_ref[...] = fn
    fn_hi = fn.astype(BF16)
    fn_lo = (fn - fn_hi.astype(F32)).astype(BF16)
    logits = _dot(jnp.concatenate([fn_hi, fn_lo, fn_hi], axis=1), rw_ref[...]) + rb_ref[...]
    lane = lax.broadcasted_iota(jnp.int32, logits.shape, 1)
    logits = jnp.where(lane < N_EXPERTS, logits, -jnp.inf)
    lane_f = lane.astype(F32)
    m1 = jnp.max(logits, axis=1, keepdims=True)
    i1 = jnp.min(jnp.where(logits == m1, lane_f, float(LANES)), axis=1, keepdims=True)
    rest = jnp.where(lane_f == i1, -jnp.inf, logits)
    m2 = jnp.max(rest, axis=1, keepdims=True)
    i2 = jnp.min(jnp.where(rest == m2, lane_f, float(LANES)), axis=1, keepdims=True)
    e2 = jnp.exp(m2 - m1)
    g1 = 1.0 / (1.0 + e2)
    g2 = e2 / (1.0 + e2)
    idx_ref[...] = jnp.where(lane == 0, i1, jnp.where(lane == 1, i2, 0.0)).astype(jnp.int32)
    gate_ref[...] = jnp.where(lane == 0, g1, jnp.where(lane == 1, g2, 0.0))


def _oproj(o, w_o, h, g, router=None, *, tm=512):
    t, d = h.shape
    row = lambda i: (i, 0)
    const = lambda i: (0, 0)
    in_specs = [
        pl.BlockSpec((tm, d), row),
        pl.BlockSpec((d, d), const),
        pl.BlockSpec((tm, d), row),
        pl.BlockSpec((1, d), const),
    ]
    if router is None:
        return pl.pallas_call(
            _oproj_kernel,
            grid=(t // tm,),
            in_specs=in_specs,
            out_specs=[pl.BlockSpec((tm, d), row), pl.BlockSpec((tm, d), row)],
            out_shape=[jax.ShapeDtypeStruct((t, d), F32), jax.ShapeDtypeStruct((t, d), BF16)],
            compiler_params=_params("parallel"),
            name="oproj_norm",
        )(o, w_o, h, g)
    rw, rb = router
    return pl.pallas_call(
        _oproj_router_kernel,
        grid=(t // tm,),
        in_specs=in_specs + [pl.BlockSpec(rw.shape, const), pl.BlockSpec((1, LANES), const)],
        out_specs=[pl.BlockSpec((tm, d), row), pl.BlockSpec((tm, d), row),
                   pl.BlockSpec((tm, LANES), row), pl.BlockSpec((tm, LANES), row)],
        out_shape=[jax.ShapeDtypeStruct((t, d), F32), jax.ShapeDtypeStruct((t, d), F32),
                   jax.ShapeDtypeStruct((t, LANES), jnp.int32), jax.ShapeDtypeStruct((t, LANES), F32)],
        compiler_params=_params("parallel"),
        name="oproj_norm_router",
    )(o, w_o, h, g, rw, rb)


def _swiglu_accumulate(x, w_gu_ref, w_down_ref, out_ref, *, chunk):
    f = w_down_ref.shape[0]
    for c0 in range(0, f, chunk):
        g = _dot(x, w_gu_ref[:, c0:c0 + chunk])
        u = _dot(x, w_gu_ref[:, f + c0:f + c0 + chunk])
        act = (g * (1.0 / (1.0 + jnp.exp(-g))) * u).astype(BF16)
        out_ref[...] += _dot(act, w_down_ref[c0:c0 + chunk, :])


def _dense_ffn_kernel(x_ref, wgu_ref, wd_ref, h_ref, o_ref, *, chunk):
    o_ref[...] = h_ref[...]
    _swiglu_accumulate(x_ref[...], wgu_ref, wd_ref, o_ref, chunk=chunk)


def _dense_ffn(xn, w_gu, w_down, h, *, tm=512, chunk=DENSE_CHUNK):
    t, d = h.shape
    f = w_down.shape[0]
    return pl.pallas_call(
        functools.partial(_dense_ffn_kernel, chunk=chunk),
        grid=(t // tm,),
        in_specs=[
            pl.BlockSpec((tm, d), lambda i: (i, 0)),
            _resident((d, 2 * f), lambda i: (0, 0)),
            _resident((f, d), lambda i: (0, 0)),
            pl.BlockSpec((tm, d), lambda i: (i, 0)),
        ],
        out_specs=pl.BlockSpec((tm, d), lambda i: (i, 0)),
        out_shape=jax.ShapeDtypeStruct((t, d), F32),
        compiler_params=_params("parallel"),
        name="dense_ffn",
    )(xn, w_gu, w_down, h)


def _ple(h2, p, wp, wg):
    gate = 1.0 / (1.0 + jnp.exp(-_dot(h2.astype(BF16), wg)))
    return h2 + _dot(p.astype(BF16), wp) * gate


def _ple_proj_kernel(h_ref, p_ref, wp_ref, wg_ref, gains_ref, wkv_ref, wq_ref, h3_ref, kv_ref, q_ref):
    h3 = _ple(h_ref[...], p_ref[...], wp_ref[...], wg_ref[...])
    h3_ref[...] = h3
    y = h3 * _rstd(h3)
    kv_ref[...] = _dot((y * gains_ref[0:1, :]).astype(BF16), wkv_ref[...]).astype(BF16)
    q = _dot((y * gains_ref[1:2, :]).astype(BF16), wq_ref[...])
    q_ref[...] = (q * (1.0 / math.sqrt(HEAD_DIM))).astype(BF16)


def _ple_proj(h2, p, wp, wg, gains, w_kv, w_q, *, tm=512):
    t, d = h2.shape
    dp = p.shape[1]
    row = lambda i: (i, 0)
    const = lambda i: (0, 0)
    return pl.pallas_call(
        _ple_proj_kernel,
        grid=(t // tm,),
        in_specs=[
            pl.BlockSpec((tm, d), row),
            pl.BlockSpec((tm, dp), row),
            _resident((dp, d), const),
            _resident((d, d), const),
            _resident(gains.shape, const),
            _resident(w_kv.shape, const),
            _resident(w_q.shape, const),
        ],
        out_specs=[pl.BlockSpec((tm, d), row), pl.BlockSpec((tm, w_kv.shape[1]), row), pl.BlockSpec((tm, d), row)],
        out_shape=[jax.ShapeDtypeStruct((t, d), F32), jax.ShapeDtypeStruct((t, w_kv.shape[1]), BF16),
                   jax.ShapeDtypeStruct((t, d), BF16)],
        compiler_params=_params("parallel"),
        name="ple_proj",
    )(h2, p, wp, wg, gains, w_kv, w_q)


def _dispatch_kernel(pend_ref, padded_ref, dest_ref, x_ref, xb_hbm, zero_ref, sem, zero_sem, *, tm):
    n_tiles = xb_hbm.shape[0] // EXPERT_TILE

    def zero_tile(tile):
        row0 = pl.multiple_of(tile * EXPERT_TILE, EXPERT_TILE)
        return pltpu.make_async_copy(zero_ref, xb_hbm.at[pl.ds(row0, EXPERT_TILE)], zero_sem)

    def last_tile(e):
        return pend_ref[e] // EXPERT_TILE - 1

    def for_unused_tiles(fn):
        def body(tile, carry):
            fn(tile)
            return carry
        lax.fori_loop(pend_ref[N_EXPERTS - 1] // EXPERT_TILE, n_tiles, body, 0)

    @pl.when(pl.program_id(0) == 0)
    def _():
        zero_ref[...] = jnp.zeros(zero_ref.shape, zero_ref.dtype)
        for e in range(N_EXPERTS):
            pl.when(padded_ref[e] > 0)(lambda e=e: zero_tile(last_tile(e)).start())
        for_unused_tiles(lambda tile: zero_tile(tile).start())
        for e in range(N_EXPERTS):
            pl.when(padded_ref[e] > 0)(lambda e=e: zero_tile(last_tile(e)).wait())
        for_unused_tiles(lambda tile: zero_tile(tile).wait())

    def issue(r, carry):
        for k in range(TOP_K):
            pltpu.make_async_copy(x_ref.at[pl.ds(r, 1)],
                                  xb_hbm.at[pl.ds(dest_ref[0, 0, r * TOP_K + k], 1)], sem).start()
        return carry

    lax.fori_loop(0, tm, issue, 0, unroll=8)
    for _ in range(TOP_K):
        pltpu.make_async_copy(x_ref, xb_hbm.at[pl.ds(0, tm)], sem).wait()


def _dispatch(x, dest3, pend, padded, n_rows, *, tm):
    t, d = x.shape
    return pl.pallas_call(
        functools.partial(_dispatch_kernel, tm=tm),
        grid_spec=pltpu.PrefetchScalarGridSpec(
            num_scalar_prefetch=2,
            grid=(t // tm,),
            in_specs=[
                pl.BlockSpec((1, 1, tm * TOP_K), lambda i, pe, pa: (i, 0, 0), memory_space=pltpu.SMEM),
                pl.BlockSpec((tm, d), lambda i, pe, pa: (i, 0)),
            ],
            out_specs=pl.BlockSpec(memory_space=pl.ANY),
            scratch_shapes=[pltpu.VMEM((EXPERT_TILE, d), x.dtype), pltpu.SemaphoreType.DMA,
                            pltpu.SemaphoreType.DMA],
        ),
        out_shape=jax.ShapeDtypeStruct((n_rows, d), x.dtype),
        compiler_params=_params("arbitrary"),
        name="moe_dispatch",
    )(pend, padded, dest3, x)


def _expert_kernel(te_ref, tv_ref, x_ref, wgu_ref, wd_ref, y_ref, *, chunk):
    del te_ref
    y_ref[...] = jnp.zeros(y_ref.shape, F32)

    @pl.when(tv_ref[pl.program_id(0)] > 0)
    def _():
        _swiglu_accumulate(x_ref[...].astype(BF16), wgu_ref.at[0], wd_ref.at[0], y_ref, chunk=chunk)


def _experts(xb, w_gu, w_down, tile_expert, tile_valid, *, chunk=EXPERT_CHUNK):
    n_rows, d = xb.shape
    fe = w_down.shape[1]
    n_tiles = n_rows // EXPERT_TILE
    return pl.pallas_call(
        functools.partial(_expert_kernel, chunk=chunk),
        grid_spec=pltpu.PrefetchScalarGridSpec(
            num_scalar_prefetch=2,
            grid=(n_tiles,),
            in_specs=[
                pl.BlockSpec((EXPERT_TILE, d), lambda ti, te, tv: (ti, 0)),
                _resident((1, d, 2 * fe), lambda ti, te, tv: (te[ti], 0, 0)),
                _resident((1, fe, d), lambda ti, te, tv: (te[ti], 0, 0)),
            ],
            out_specs=pl.BlockSpec((EXPERT_TILE, d), lambda ti, te, tv: (ti, 0)),
        ),
        out_shape=jax.ShapeDtypeStruct((n_rows, d), F32),
        compiler_params=_params("arbitrary"),
        name="moe_experts",
    )(tile_expert, tile_valid, xb, w_gu, w_down)


def _combine_kernel(dest_ref, dest_next_ref, yb_hbm, h_ref, gate_ref, p_ref, wp_ref, wg_ref, g_ref, o_ref,
                    y_ref, sems, *, tm):
    i = pl.program_id(0)
    slot = i % 2

    def gather(d_ref, s):
        def issue(r, carry):
            for k in range(TOP_K):
                pltpu.make_async_copy(yb_hbm.at[pl.ds(d_ref[0, 0, r * TOP_K + k], 1)],
                                      y_ref.at[s, k, pl.ds(r, 1)], sems.at[s]).start()
            return carry
        lax.fori_loop(0, tm, issue, 0, unroll=8)

    pl.when(i == 0)(lambda: gather(dest_ref, 0))
    pl.when(i + 1 < pl.num_programs(0))(lambda: gather(dest_next_ref, 1 - slot))
    for k in range(TOP_K):
        pltpu.make_async_copy(yb_hbm.at[pl.ds(0, tm)], y_ref.at[slot, k], sems.at[slot]).wait()

    gates = gate_ref[...]
    h2 = h_ref[...] + (y_ref[slot, 0] * gates[:, 0:1] + y_ref[slot, 1] * gates[:, 1:2])
    h3 = _ple(h2, p_ref[...], wp_ref[...], wg_ref[...])
    o_ref[...] = h3 * _rstd(h3) * g_ref[...]


def _combine(dest3, yb, h, gates, p, wp, wg, g, *, tm):
    t, d = h.shape
    dp = p.shape[1]
    n = t // tm
    row = lambda i: (i, 0)
    const = lambda i: (0, 0)
    dest_spec = lambda index_map: pl.BlockSpec((1, 1, tm * TOP_K), index_map, memory_space=pltpu.SMEM)
    return pl.pallas_call(
        functools.partial(_combine_kernel, tm=tm),
        grid=(n,),
        in_specs=[
            dest_spec(lambda i: (i, 0, 0)),
            dest_spec(lambda i: (jnp.minimum(i + 1, n - 1), 0, 0)),
            pl.BlockSpec(memory_space=pl.ANY),
            pl.BlockSpec((tm, d), row),
            pl.BlockSpec((tm, LANES), row),
            pl.BlockSpec((tm, dp), row),
            _resident((dp, d), const),
            _resident((d, d), const),
            _resident((1, d), const),
        ],
        out_specs=pl.BlockSpec((tm, d), row),
        out_shape=jax.ShapeDtypeStruct((t, d), F32),
        scratch_shapes=[pltpu.VMEM((2, TOP_K, tm, d), F32), pltpu.SemaphoreType.DMA((2,))],
        compiler_params=_params("arbitrary"),
        name="moe_combine",
    )(dest3, dest3, yb, h, gates, p, wp, wg, g)


def _routing_tables(idx, n_tokens, *, tm):
    e_flat = idx.reshape(-1)
    onehot = (e_flat[:, None] == jnp.arange(N_EXPERTS, dtype=jnp.int32)[None, :]).astype(jnp.int32)
    csum = jnp.cumsum(onehot, axis=0)
    counts = csum[-1]
    padded = (counts + EXPERT_TILE - 1) // EXPERT_TILE * EXPERT_TILE
    pend = jnp.cumsum(padded)
    pstart = pend - padded
    dest = jnp.sum(onehot * (csum - 1 + pstart[None, :]), axis=1)
    n_tiles = (n_tokens * TOP_K) // EXPERT_TILE + N_EXPERTS
    tile_row0 = jnp.arange(n_tiles, dtype=jnp.int32) * EXPERT_TILE
    tile_expert = jnp.minimum(jnp.sum((tile_row0[:, None] >= pend[None, :]).astype(jnp.int32), axis=1),
                              N_EXPERTS - 1)
    tile_valid = (tile_row0 < pend[-1]).astype(jnp.int32)
    dest3 = dest.astype(jnp.int32).reshape(n_tokens // tm, 1, tm * TOP_K)
    return dest3, tile_expert, tile_valid, pend.astype(jnp.int32), padded.astype(jnp.int32), n_tiles


def kernel(x, p, attn_norm, ffn_norm, w_in_a, b_f, w_o_a, kv_norm, w_kv, w_q_b, w_o_b, w_gu_dense, w_down_dense, router_w, router_b, w_gu_moe, w_down_moe, w_ple_proj, w_ple_gate, final_norm):
    batch, seq, d = x.shape
    t = batch * seq
    n_heads = d // HEAD_DIM
    x2 = x.reshape(t, d)
    p2 = p.reshape(p.shape[0], t, p.shape[-1])
    bf = lambda w: w.astype(BF16)

    ii = lax.broadcasted_iota(jnp.int32, (CUMSUM_CHUNK, CUMSUM_CHUNK), 0)
    jj = lax.broadcasted_iota(jnp.int32, (CUMSUM_CHUNK, CUMSUM_CHUNK), 1)
    tri_le = (ii <= jj).astype(BF16)
    tri_ge = (ii >= jj).astype(BF16)
    tri_ge2 = jnp.concatenate([tri_ge, tri_ge], axis=0)

    w_in = w_in_a[0]
    qkv, lf_t = _qkvf_proj(x2, attn_norm[0:1], bf(w_in[:, :3 * d]), bf(w_in[:, 3 * d:].T),
                           b_f[0].reshape(n_heads, 1))
    c = _forget_cumsum(lf_t, tri_le, seq=seq)
    o0 = _fox_attention(qkv, c, batch=batch, seq=seq, d_model=d)
    h1, fn0 = _oproj(o0, bf(w_o_a[0]), x2, ffn_norm[0:1])
    h2 = _dense_ffn(fn0, bf(w_gu_dense[0]), bf(w_down_dense[0]), h1)
    gains = jnp.stack([kv_norm, attn_norm[1]])
    h3, kv, q1 = _ple_proj(h2, p2[0], bf(w_ple_proj[0]), bf(w_ple_gate[0]), gains, bf(w_kv), bf(w_q_b[0]))

    o1 = _sb_attention(q1, kv, tri_ge2, batch=batch, seq=seq, d_model=d)
    rw = jnp.pad(router_w[0], ((0, 0), (0, LANES - N_EXPERTS)))
    rw_hi = bf(rw)
    rw_lo = bf(rw - rw_hi.astype(F32))
    rw = jnp.concatenate([rw_hi, rw_hi, rw_lo], axis=0)
    rb = jnp.pad(router_b[0], (0, LANES - N_EXPERTS)).reshape(1, LANES)
    h4, fn1, idx, gates = _oproj(o1, bf(w_o_b[0]), h3, ffn_norm[1:2], router=(rw, rb))

    tm = 512
    dest3, tile_expert, tile_valid, pend, padded, n_tiles = _routing_tables(idx[:, :TOP_K], t, tm=tm)
    xb = _dispatch(fn1, dest3, pend, padded, n_tiles * EXPERT_TILE, tm=tm)
    yb = _experts(xb, bf(w_gu_moe[0]), bf(w_down_moe[0]), tile_expert, tile_valid)
    out = _combine(dest3, yb, h4, gates, p2[1], bf(w_ple_proj[1]), bf(w_ple_gate[1]),
                   final_norm.reshape(1, d), tm=tm)
    return out.reshape(batch, seq, d)
```

```python
import functools
import math

import jax
import jax.numpy as jnp
from jax import lax
from jax.experimental import pallas as pl
from jax.experimental.pallas import tpu as pltpu

HEAD_DIM = 64
LANES = 128
HEADS_PER_BLOCK = LANES // HEAD_DIM
N_EXPERTS = 8
TOP_K = 2
RMS_EPS = 1e-6
LOG2_E = 1.4426950408889634
ATTN_TILE = 256
CUMSUM_CHUNK = 256
EXPERT_TILE = 512
DENSE_CHUNK = 256
EXPERT_CHUNK = 512
VMEM_LIMIT = 56 * 1024 * 1024

BF16 = jnp.bfloat16
F32 = jnp.float32


def _params(*semantics):
    return pltpu.CompilerParams(dimension_semantics=semantics, vmem_limit_bytes=VMEM_LIMIT)


def _dot(a, b):
    return jnp.dot(a, b, preferred_element_type=F32)


def _dot_nt(a, b):
    return lax.dot_general(a, b, (((1,), (1,)), ((), ())), preferred_element_type=F32)


def _rstd(x):
    return lax.rsqrt(jnp.mean(x * x, axis=-1, keepdims=True) + RMS_EPS)


def _log_sigmoid(u):
    return jnp.minimum(u, 0.0) - jnp.log1p(jnp.exp(-jnp.abs(u)))


def _split3(x):
    hi = x.astype(BF16)
    r1 = x - hi.astype(F32)
    mid = r1.astype(BF16)
    lo = (r1 - mid.astype(F32)).astype(BF16)
    return hi, mid, lo


def _resident(block_shape, index_map):
    return pl.BlockSpec(block_shape, index_map, pipeline_mode=pl.Buffered(1))


def _qkvf_kernel(x_ref, g_ref, w_ref, wf_ref, bf_ref, qkv_ref, lf_ref):
    d = x_ref.shape[1]
    x = x_ref[...]
    xn = (x * _rstd(x) * g_ref[...]).astype(BF16)
    f = _dot_nt(wf_ref[...], xn) + bf_ref[...]
    lf_ref[...] = _log_sigmoid(f)
    for part in range(3):
        cols = slice(part * d, (part + 1) * d)
        y = _dot(xn, w_ref[:, cols])
        if part == 0:
            y = y * (1.0 / math.sqrt(HEAD_DIM))
        qkv_ref[:, cols] = y.astype(BF16)


def _qkvf_proj(x, g, w_qkv, w_f_t, b_f, *, tm=512):
    t, d = x.shape
    n = w_qkv.shape[1]
    h = w_f_t.shape[0]
    const = lambda i: (0, 0)
    return pl.pallas_call(
        _qkvf_kernel,
        grid=(t // tm,),
        in_specs=[
            pl.BlockSpec((tm, d), lambda i: (i, 0)),
            _resident((1, d), const),
            _resident((d, n), const),
            _resident((h, d), const),
            _resident((h, 1), const),
        ],
        out_specs=[
            pl.BlockSpec((tm, n), lambda i: (i, 0)),
            pl.BlockSpec((h, tm), lambda i: (0, i)),
        ],
        out_shape=[
            jax.ShapeDtypeStruct((t, n), BF16),
            jax.ShapeDtypeStruct((h, t), F32),
        ],
        compiler_params=_params("parallel"),
        name="qkvf_proj",
    )(x, g, w_qkv, w_f_t, b_f)


def _cumsum_kernel(lf_ref, tri_ref, c_ref):
    h, s = lf_ref.shape
    tri = tri_ref[...]
    carry = jnp.zeros((h, 1), F32)
    for c0 in range(0, s, CUMSUM_CHUNK):
        hi, mid, lo = _split3(lf_ref[:, c0:c0 + CUMSUM_CHUNK])
        cs = _dot(lo, tri) + _dot(mid, tri) + _dot(hi, tri) + carry
        for hd in range(h):
            c_ref[hd, :, c0:c0 + CUMSUM_CHUNK] = cs[hd:hd + 1, :]
        carry = cs[:, CUMSUM_CHUNK - 1:CUMSUM_CHUNK]


def _forget_cumsum(lf_t, tri, *, seq):
    h, t = lf_t.shape
    return pl.pallas_call(
        _cumsum_kernel,
        grid=(t // seq,),
        in_specs=[
            pl.BlockSpec((h, seq), lambda b: (0, b)),
            pl.BlockSpec((CUMSUM_CHUNK, CUMSUM_CHUNK), lambda b: (0, 0)),
        ],
        out_specs=pl.BlockSpec((h, 1, seq), lambda b: (0, 0, b)),
        out_shape=jax.ShapeDtypeStruct((h, 1, t), F32),
        compiler_params=_params("parallel"),
        name="forget_cumsum",
    )(lf_t, tri)


def _head_masks(shape):
    lane = lax.broadcasted_iota(jnp.int32, shape, 1)
    return lane < HEAD_DIM


def _split_heads(x2, first_head_lanes, fill):
    return (jnp.where(first_head_lanes, x2, fill), jnp.where(first_head_lanes, fill, x2))


def _stack_heads(q2, first_head_lanes):
    return jnp.concatenate(_split_heads(q2, first_head_lanes, jnp.zeros_like(q2)), axis=0)


def _lane_tile(x, width):
    return jnp.concatenate([x] * (width // LANES), axis=1)


def _fox_kernel(q_ref, k_ref, v_ref, c_ref, o_ref, *scratch):
    tq = ATTN_TILE
    nq = q_ref.shape[0] // tq
    m_refs, acc_refs = scratch[:nq], scratch[nq:]
    first = _head_masks((tq, LANES))

    def tile(qs, m_ref, acc_ref, kt, diagonal):
        k = k_ref[kt * tq:(kt + 1) * tq, :]
        v = v_ref[kt * tq:(kt + 1) * tq, :]
        v_one = jnp.concatenate([v, jnp.ones_like(v)], axis=1)
        s = _dot_nt(qs, k)
        s = jnp.concatenate([s[hd * tq:(hd + 1) * tq] - c_ref[hd, :, kt * tq:(kt + 1) * tq]
                             for hd in range(HEADS_PER_BLOCK)], axis=0)
        if diagonal:
            row = lax.broadcasted_iota(jnp.int32, s.shape, 0) & (tq - 1)
            col = lax.broadcasted_iota(jnp.int32, s.shape, 1)
            s = jnp.where(col <= row, s, -jnp.inf)
        m_prev = m_ref[...]
        m_new = jnp.maximum(m_prev, jnp.max(s, axis=1, keepdims=True))
        alpha = jnp.exp(m_prev - m_new)
        p = jnp.exp(s - _lane_tile(m_new, tq))
        acc_ref[...] = _lane_tile(alpha, 2 * LANES) * acc_ref[...] + _dot(p.astype(BF16), v_one)
        m_ref[...] = m_new

    for qt in range(nq):
        qs = _stack_heads(q_ref[qt * tq:(qt + 1) * tq, :], first)
        m_ref, acc_ref = m_refs[qt], acc_refs[qt]
        m_ref[...] = jnp.full(m_ref.shape, -jnp.inf, F32)
        acc_ref[...] = jnp.zeros(acc_ref.shape, F32)
        for kt in range(qt + 1):
            tile(qs, m_ref, acc_ref, kt, kt == qt)
        out = acc_ref[:, :LANES] / acc_ref[:, LANES:]
        o_ref[qt * tq:(qt + 1) * tq, :] = jnp.where(first, out[:tq], out[tq:]).astype(BF16)


def _fox_attention(qkv, c, *, batch, seq, d_model):
    t = qkv.shape[0]
    n_pairs = d_model // LANES
    tq = ATTN_TILE
    nq = seq // tq
    assert tq & (tq - 1) == 0, "the diagonal mask takes row % tq as row & (tq - 1)"
    rows = HEADS_PER_BLOCK * tq
    return pl.pallas_call(
        _fox_kernel,
        grid=(batch, n_pairs),
        in_specs=[
            pl.BlockSpec((seq, LANES), lambda b, hp: (b, hp)),
            pl.BlockSpec((seq, LANES), lambda b, hp: (b, n_pairs + hp)),
            pl.BlockSpec((seq, LANES), lambda b, hp: (b, 2 * n_pairs + hp)),
            pl.BlockSpec((HEADS_PER_BLOCK, 1, seq), lambda b, hp: (hp, 0, b)),
        ],
        out_specs=pl.BlockSpec((seq, LANES), lambda b, hp: (b, hp)),
        out_shape=jax.ShapeDtypeStruct((t, d_model), BF16),
        scratch_shapes=([pltpu.VMEM((rows, LANES), F32)] * nq + [pltpu.VMEM((rows, 2 * LANES), F32)] * nq),
        compiler_params=_params("parallel", "parallel"),
        name="fox_attention",
    )(qkv, qkv, qkv, c)


def _sb_kernel(q_ref, k_ref, v_ref, tri_ref, o_ref, *scratch):
    tq = ATTN_TILE
    nq = q_ref.shape[0] // tq
    r_refs, acc_refs = scratch[:nq], scratch[nq:]
    sub = CUMSUM_CHUNK
    n_sub = tq // sub
    first = _head_masks((tq, LANES))

    def tile(qs, r_ref, acc_ref, kt, diagonal):
        k = k_ref[kt * tq:(kt + 1) * tq, :]
        v = v_ref[kt * tq:(kt + 1) * tq, :]
        z = _dot_nt(qs, k)
        nlm = jnp.maximum(z, 0.0) + jnp.log(1.0 + jnp.exp2(jnp.abs(z) * (-LOG2_E)))
        if diagonal:
            row = lax.broadcasted_iota(jnp.int32, z.shape, 0) & (tq - 1)
            col = lax.broadcasted_iota(jnp.int32, z.shape, 1)
            visible = col < row
            nlm = jnp.where(visible, nlm, 0.0)
        nlm16 = nlm.astype(BF16)
        p = [_dot(nlm16[:, sb * sub:(sb + 1) * sub], tri_ref[...]) for sb in range(n_sub)]
        r = r_ref[...]
        later = [None] * n_sub
        for sb in reversed(range(n_sub)):
            later[sb] = p[sb] + _lane_tile(r, sub)
            r = r + p[sb][:, 0:1]
        w = jnp.exp(z - jnp.concatenate(later, axis=1))
        if diagonal:
            w = jnp.where(visible, w, 0.0)
        acc_ref[...] += _dot(w.astype(BF16), v)
        r_ref[...] = r

    for qt in range(nq):
        qs = _stack_heads(q_ref[qt * tq:(qt + 1) * tq, :], first)
        r_ref, acc_ref = r_refs[qt], acc_refs[qt]
        r_ref[...] = jnp.zeros(r_ref.shape, F32)
        acc_ref[...] = jnp.zeros(acc_ref.shape, F32)
        for kt in reversed(range(qt + 1)):
            tile(qs, r_ref, acc_ref, kt, kt == qt)
        o_ref[qt * tq:(qt + 1) * tq, :] = jnp.where(first, acc_ref[:tq], acc_ref[tq:]).astype(BF16)


def _sb_attention(q, kv, tri, *, batch, seq, d_model):
    t = q.shape[0]
    n_pairs = d_model // LANES
    tq = ATTN_TILE
    nq = seq // tq
    assert tq & (tq - 1) == 0, "the diagonal mask takes row % tq as row & (tq - 1)"
    stat = pltpu.VMEM((HEADS_PER_BLOCK * tq, LANES), F32)
    return pl.pallas_call(
        _sb_kernel,
        grid=(batch, n_pairs),
        in_specs=[
            pl.BlockSpec((seq, LANES), lambda b, hp: (b, hp)),
            pl.BlockSpec((seq, LANES), lambda b, hp: (b, hp)),
            pl.BlockSpec((seq, LANES), lambda b, hp: (b, n_pairs + hp)),
            pl.BlockSpec(tri.shape, lambda b, hp: (0, 0)),
        ],
        out_specs=pl.BlockSpec((seq, LANES), lambda b, hp: (b, hp)),
        out_shape=jax.ShapeDtypeStruct((t, d_model), BF16),
        scratch_shapes=[stat] * (2 * nq),
        compiler_params=_params("parallel", "parallel"),
        name="sb_attention",
    )(q, kv, kv, tri)


def _oproj_router_kernel(o_ref, w_ref, h_ref, g_ref, rw_ref, rb_ref, h1_ref, fn_ref, idx_ref, gate_ref):
    h1 = h_ref[...] + _dot(o_ref[...], w_ref[...])
    h1_ref[...] = h1
    fn = h1 * _rstd(h1) * g_ref[...]
    fn_ref[...] = fn
    fn_hi = fn.astype(BF16)
    fn_lo = (fn - fn_hi.astype(F32)).astype(BF16)
    logits = _dot(jnp.concatenate([fn_hi, fn_lo, fn_hi], axis=1), rw_ref[...]) + rb_ref[...]
    lane = lax.broadcasted_iota(jnp.int32, logits.shape, 1)
    logits = jnp.where(lane < N_EXPERTS, logits, -jnp.inf)
    lane_f = lane.astype(F32)
    m1 = jnp.max(logits, axis=1, keepdims=True)
    i1 = jnp.min(jnp.where(logits == m1, lane_f, float(LANES)), axis=1, keepdims=True)
    rest = jnp.where(lane_f == i1, -jnp.inf, logits)
    m2 = jnp.max(rest, axis=1, keepdims=True)
    i2 = jnp.min(jnp.where(rest == m2, lane_f, float(LANES)), axis=1, keepdims=True)
    e2 = jnp.exp(m2 - m1)
    g1 = 1.0 / (1.0 + e2)
    g2 = e2 / (1.0 + e2)
    idx_ref[...] = jnp.where(lane == 0, i1, jnp.where(lane == 1, i2, 0.0)).astype(jnp.int32)
    gate_ref[...] = jnp.where(lane == 0, g1, jnp.where(lane == 1, g2, 0.0))


def _oproj_router(o, w_o, h, g, rw, rb, *, tm=512):
    t, d = h.shape
    row = lambda i: (i, 0)
    const = lambda i: (0, 0)
    return pl.pallas_call(
        _oproj_router_kernel,
        grid=(t // tm,),
        in_specs=[pl.BlockSpec((tm, d), row), _resident((d, d), const), pl.BlockSpec((tm, d), row),
                  _resident((1, d), const), _resident(rw.shape, const), _resident((1, LANES), const)],
        out_specs=[pl.BlockSpec((tm, d), row), pl.BlockSpec((tm, d), row),
                   pl.BlockSpec((tm, LANES), row), pl.BlockSpec((tm, LANES), row)],
        out_shape=[jax.ShapeDtypeStruct((t, d), F32), jax.ShapeDtypeStruct((t, d), F32),
                   jax.ShapeDtypeStruct((t, LANES), jnp.int32), jax.ShapeDtypeStruct((t, LANES), F32)],
        compiler_params=_params("parallel"),
        name="oproj_norm_router",
    )(o, w_o, h, g, rw, rb)


def _swiglu_accumulate(x, w_gu_ref, w_down_ref, out_ref, *, chunk):
    f = w_down_ref.shape[0]
    for c0 in range(0, f, chunk):
        g = _dot(x, w_gu_ref[:, c0:c0 + chunk])
        u = _dot(x, w_gu_ref[:, f + c0:f + c0 + chunk])
        act = (g * (1.0 / (1.0 + jnp.exp(-g))) * u).astype(BF16)
        out_ref[...] += _dot(act, w_down_ref[c0:c0 + chunk, :])


def _ple(h2, p, wp, wg):
    gate = 1.0 / (1.0 + jnp.exp(-_dot(h2.astype(BF16), wg)))
    return h2 + _dot(p.astype(BF16), wp) * gate


def _layer0_tail_kernel(o_ref, x_ref, p_ref, wo_ref, g_ref, wgu_ref, wd_ref, wp_ref, wg_ref, gains_ref,
                        wkv_ref, wq_ref, h3_ref, kv_ref, q_ref):
    h1 = x_ref[...] + _dot(o_ref[...], wo_ref[...])
    fn = (h1 * _rstd(h1) * g_ref[...]).astype(BF16)
    h3_ref[...] = h1
    _swiglu_accumulate(fn, wgu_ref, wd_ref, h3_ref, chunk=DENSE_CHUNK)
    h3 = _ple(h3_ref[...], p_ref[...], wp_ref[...], wg_ref[...])
    h3_ref[...] = h3
    y = h3 * _rstd(h3)
    kv_ref[...] = _dot((y * gains_ref[0:1, :]).astype(BF16), wkv_ref[...]).astype(BF16)
    q = _dot((y * gains_ref[1:2, :]).astype(BF16), wq_ref[...])
    q_ref[...] = (q * (1.0 / math.sqrt(HEAD_DIM))).astype(BF16)


def _layer0_tail(o, x, p, w_o, g, w_gu, w_down, wp, wg, gains, w_kv, w_q, *, tm=512):
    t, d = x.shape
    dp = p.shape[1]
    row = lambda i: (i, 0)
    const = lambda i: (0, 0)
    weights = (w_o, g, w_gu, w_down, wp, wg, gains, w_kv, w_q)
    return pl.pallas_call(
        _layer0_tail_kernel,
        grid=(t // tm,),
        in_specs=[pl.BlockSpec((tm, d), row), pl.BlockSpec((tm, d), row), pl.BlockSpec((tm, dp), row)]
                 + [_resident(w.shape, const) for w in weights],
        out_specs=[pl.BlockSpec((tm, d), row), pl.BlockSpec((tm, w_kv.shape[1]), row), pl.BlockSpec((tm, d), row)],
        out_shape=[jax.ShapeDtypeStruct((t, d), F32), jax.ShapeDtypeStruct((t, w_kv.shape[1]), BF16),
                   jax.ShapeDtypeStruct((t, d), BF16)],
        compiler_params=_params("parallel"),
        name="layer0_tail",
    )(o, x, p, *weights)


def _dispatch_kernel(pend_ref, padded_ref, dest_ref, x_ref, xb_hbm, zero_ref, sem, zero_sem, *, tm):
    n_tiles = xb_hbm.shape[0] // EXPERT_TILE

    def zero_tile(tile):
        row0 = pl.multiple_of(tile * EXPERT_TILE, EXPERT_TILE)
        return pltpu.make_async_copy(zero_ref, xb_hbm.at[pl.ds(row0, EXPERT_TILE)], zero_sem)

    def last_tile(e):
        return pend_ref[e] // EXPERT_TILE - 1

    def for_unused_tiles(fn):
        def body(tile, carry):
            fn(tile)
            return carry
        lax.fori_loop(pend_ref[N_EXPERTS - 1] // EXPERT_TILE, n_tiles, body, 0)

    @pl.when(pl.program_id(0) == 0)
    def _():
        zero_ref[...] = jnp.zeros(zero_ref.shape, zero_ref.dtype)
        for e in range(N_EXPERTS):
            pl.when(padded_ref[e] > 0)(lambda e=e: zero_tile(last_tile(e)).start())
        for_unused_tiles(lambda tile: zero_tile(tile).start())
        for e in range(N_EXPERTS):
            pl.when(padded_ref[e] > 0)(lambda e=e: zero_tile(last_tile(e)).wait())
        for_unused_tiles(lambda tile: zero_tile(tile).wait())

    def issue(r, carry):
        for k in range(TOP_K):
            pltpu.make_async_copy(x_ref.at[pl.ds(r, 1)],
                                  xb_hbm.at[pl.ds(dest_ref[0, 0, r * TOP_K + k], 1)], sem).start()
        return carry

    lax.fori_loop(0, tm, issue, 0, unroll=8)
    for _ in range(TOP_K):
        pltpu.make_async_copy(x_ref, xb_hbm.at[pl.ds(0, tm)], sem).wait()


def _dispatch(x, dest3, pend, padded, n_rows, *, tm):
    t, d = x.shape
    return pl.pallas_call(
        functools.partial(_dispatch_kernel, tm=tm),
        grid_spec=pltpu.PrefetchScalarGridSpec(
            num_scalar_prefetch=2,
            grid=(t // tm,),
            in_specs=[
                pl.BlockSpec((1, 1, tm * TOP_K), lambda i, pe, pa: (i, 0, 0), memory_space=pltpu.SMEM),
                pl.BlockSpec((tm, d), lambda i, pe, pa: (i, 0)),
            ],
            out_specs=pl.BlockSpec(memory_space=pl.ANY),
            scratch_shapes=[pltpu.VMEM((EXPERT_TILE, d), x.dtype), pltpu.SemaphoreType.DMA,
                            pltpu.SemaphoreType.DMA],
        ),
        out_shape=jax.ShapeDtypeStruct((n_rows, d), x.dtype),
        compiler_params=_params("arbitrary"),
        name="moe_dispatch",
    )(pend, padded, dest3, x)


def _expert_kernel(te_ref, tv_ref, x_ref, wgu_ref, wd_ref, y_ref, *, chunk):
    del te_ref
    y_ref[...] = jnp.zeros(y_ref.shape, F32)

    @pl.when(tv_ref[pl.program_id(0)] > 0)
    def _():
        _swiglu_accumulate(x_ref[...].astype(BF16), wgu_ref.at[0], wd_ref.at[0], y_ref, chunk=chunk)


def _experts(xb, w_gu, w_down, tile_expert, tile_valid, *, chunk=EXPERT_CHUNK):
    n_rows, d = xb.shape
    fe = w_down.shape[1]
    n_tiles = n_rows // EXPERT_TILE
    return pl.pallas_call(
        functools.partial(_expert_kernel, chunk=chunk),
        grid_spec=pltpu.PrefetchScalarGridSpec(
            num_scalar_prefetch=2,
            grid=(n_tiles,),
            in_specs=[
                pl.BlockSpec((EXPERT_TILE, d), lambda ti, te, tv: (ti, 0)),
                _resident((1, d, 2 * fe), lambda ti, te, tv: (te[ti], 0, 0)),
                _resident((1, fe, d), lambda ti, te, tv: (te[ti], 0, 0)),
            ],
            out_specs=pl.BlockSpec((EXPERT_TILE, d), lambda ti, te, tv: (ti, 0)),
        ),
        out_shape=jax.ShapeDtypeStruct((n_rows, d), F32),
        compiler_params=_params("arbitrary"),
        name="moe_experts",
    )(tile_expert, tile_valid, xb, w_gu, w_down)


def _combine_kernel(dest_ref, dest_next_ref, yb_hbm, h_ref, gate_ref, p_ref, wp_ref, wg_ref, g_ref, o_ref,
                    y_ref, sems, *, tm):
    i = pl.program_id(0)
    slot = i % 2

    def gather(d_ref, s):
        def issue(r, carry):
            for k in range(TOP_K):
                pltpu.make_async_copy(yb_hbm.at[pl.ds(d_ref[0, 0, r * TOP_K + k], 1)],
                                      y_ref.at[s, k, pl.ds(r, 1)], sems.at[s]).start()
            return carry
        lax.fori_loop(0, tm, issue, 0, unroll=8)

    pl.when(i == 0)(lambda: gather(dest_ref, 0))
    pl.when(i + 1 < pl.num_programs(0))(lambda: gather(dest_next_ref, 1 - slot))
    for k in range(TOP_K):
        pltpu.make_async_copy(yb_hbm.at[pl.ds(0, tm)], y_ref.at[slot, k], sems.at[slot]).wait()

    gates = gate_ref[...]
    h2 = h_ref[...] + (y_ref[slot, 0] * gates[:, 0:1] + y_ref[slot, 1] * gates[:, 1:2])
    h3 = _ple(h2, p_ref[...], wp_ref[...], wg_ref[...])
    o_ref[...] = h3 * _rstd(h3) * g_ref[...]


def _combine(dest3, yb, h, gates, p, wp, wg, g, *, tm):
    t, d = h.shape
    dp = p.shape[1]
    n = t // tm
    row = lambda i: (i, 0)
    const = lambda i: (0, 0)
    dest_spec = lambda index_map: pl.BlockSpec((1, 1, tm * TOP_K), index_map, memory_space=pltpu.SMEM)
    return pl.pallas_call(
        functools.partial(_combine_kernel, tm=tm),
        grid=(n,),
        in_specs=[
            dest_spec(lambda i: (i, 0, 0)),
            dest_spec(lambda i: (jnp.minimum(i + 1, n - 1), 0, 0)),
            pl.BlockSpec(memory_space=pl.ANY),
            pl.BlockSpec((tm, d), row),
            pl.BlockSpec((tm, LANES), row),
            pl.BlockSpec((tm, dp), row),
            _resident((dp, d), const),
            _resident((d, d), const),
            _resident((1, d), const),
        ],
        out_specs=pl.BlockSpec((tm, d), row),
        out_shape=jax.ShapeDtypeStruct((t, d), F32),
        scratch_shapes=[pltpu.VMEM((2, TOP_K, tm, d), F32), pltpu.SemaphoreType.DMA((2,))],
        compiler_params=_params("arbitrary"),
        name="moe_combine",
    )(dest3, dest3, yb, h, gates, p, wp, wg, g)


def _routing_tables(idx, n_tokens, *, tm):
    e_flat = idx.reshape(-1)
    onehot = (e_flat[:, None] == jnp.arange(N_EXPERTS, dtype=jnp.int32)[None, :]).astype(jnp.int32)
    csum = jnp.cumsum(onehot, axis=0)
    counts = csum[-1]
    padded = (counts + EXPERT_TILE - 1) // EXPERT_TILE * EXPERT_TILE
    pend = jnp.cumsum(padded)
    pstart = pend - padded
    dest = jnp.sum(onehot * (csum - 1 + pstart[None, :]), axis=1)
    n_tiles = (n_tokens * TOP_K) // EXPERT_TILE + N_EXPERTS
    tile_row0 = jnp.arange(n_tiles, dtype=jnp.int32) * EXPERT_TILE
    tile_expert = jnp.minimum(jnp.sum((tile_row0[:, None] >= pend[None, :]).astype(jnp.int32), axis=1),
                              N_EXPERTS - 1)
    tile_valid = (tile_row0 < pend[-1]).astype(jnp.int32)
    dest3 = dest.astype(jnp.int32).reshape(n_tokens // tm, 1, tm * TOP_K)
    return dest3, tile_expert, tile_valid, pend.astype(jnp.int32), padded.astype(jnp.int32), n_tiles


def kernel(x, p, attn_norm, ffn_norm, w_in_a, b_f, w_o_a, kv_norm, w_kv, w_q_b, w_o_b, w_gu_dense, w_down_dense, router_w, router_b, w_gu_moe, w_down_moe, w_ple_proj, w_ple_gate, final_norm):
    batch, seq, d = x.shape
    t = batch * seq
    n_heads = d // HEAD_DIM
    x2 = x.reshape(t, d)
    p2 = p.reshape(p.shape[0], t, p.shape[-1])
    bf = lambda w: w.astype(BF16)

    ii = lax.broadcasted_iota(jnp.int32, (CUMSUM_CHUNK, CUMSUM_CHUNK), 0)
    jj = lax.broadcasted_iota(jnp.int32, (CUMSUM_CHUNK, CUMSUM_CHUNK), 1)
    tri_le = (ii <= jj).astype(BF16)
    tri_ge = (ii >= jj).astype(BF16)

    w_in = w_in_a[0]
    qkv, lf_t = _qkvf_proj(x2, attn_norm[0:1], bf(w_in[:, :3 * d]), bf(w_in[:, 3 * d:].T),
                           b_f[0].reshape(n_heads, 1))
    c = _forget_cumsum(lf_t, tri_le, seq=seq)
    o0 = _fox_attention(qkv, c, batch=batch, seq=seq, d_model=d)
    gains = jnp.stack([kv_norm, attn_norm[1]])
    h3, kv, q1 = _layer0_tail(o0, x2, p2[0], bf(w_o_a[0]), ffn_norm[0:1], bf(w_gu_dense[0]), bf(w_down_dense[0]),
                              bf(w_ple_proj[0]), bf(w_ple_gate[0]), gains, bf(w_kv), bf(w_q_b[0]))

    o1 = _sb_attention(q1, kv, tri_ge, batch=batch, seq=seq, d_model=d)
    rw = jnp.pad(router_w[0], ((0, 0), (0, LANES - N_EXPERTS)))
    rw_hi = bf(rw)
    rw_lo = bf(rw - rw_hi.astype(F32))
    rw = jnp.concatenate([rw_hi, rw_hi, rw_lo], axis=0)
    rb = jnp.pad(router_b[0], (0, LANES - N_EXPERTS)).reshape(1, LANES)
    h4, fn1, idx, gates = _oproj_router(o1, bf(w_o_b[0]), h3, ffn_norm[1:2], rw, rb)

    tm = 512
    dest3, tile_expert, tile_valid, pend, padded, n_tiles = _routing_tables(idx[:, :TOP_K], t, tm=tm)
    xb = _dispatch(fn1, dest3, pend, padded, n_tiles * EXPERT_TILE, tm=tm)
    yb = _experts(xb, bf(w_gu_moe[0]), bf(w_down_moe[0]), tile_expert, tile_valid)
    out = _combine(dest3, yb, h4, gates, p2[1], bf(w_ple_proj[1]), bf(w_ple_gate[1]),
                   final_norm.reshape(1, d), tm=tm)
    return out.reshape(batch, seq, d)
```

```python
import functools
import math

import jax
import jax.numpy as jnp
from jax import lax
from jax.experimental import pallas as pl
from jax.experimental.pallas import tpu as pltpu

HEAD_DIM = 64
LANES = 128
HEADS_PER_BLOCK = LANES // HEAD_DIM
N_EXPERTS = 8
TOP_K = 2
RMS_EPS = 1e-6
LOG2_E = 1.4426950408889634
ATTN_TILE = 256
CUMSUM_CHUNK = 256
EXPERT_TILE = 512
DENSE_CHUNK = 256
EXPERT_CHUNK = 512
VMEM_LIMIT = 56 * 1024 * 1024

BF16 = jnp.bfloat16
F32 = jnp.float32


def _params(*semantics):
    return pltpu.CompilerParams(dimension_semantics=semantics, vmem_limit_bytes=VMEM_LIMIT)


def _dot(a, b):
    return jnp.dot(a, b, preferred_element_type=F32)


def _dot_nt(a, b):
    return lax.dot_general(a, b, (((1,), (1,)), ((), ())), preferred_element_type=F32)


def _rstd(x):
    return lax.rsqrt(jnp.mean(x * x, axis=-1, keepdims=True) + RMS_EPS)


def _log_sigmoid(u):
    return jnp.minimum(u, 0.0) - jnp.log1p(jnp.exp(-jnp.abs(u)))


def _split3(x):
    hi = x.astype(BF16)
    r1 = x - hi.astype(F32)
    mid = r1.astype(BF16)
    lo = (r1 - mid.astype(F32)).astype(BF16)
    return hi, mid, lo


def _resident(block_shape, index_map):
    return pl.BlockSpec(block_shape, index_map, pipeline_mode=pl.Buffered(1))


def _qkvf_kernel(x_ref, g_ref, w_ref, wf_ref, bf_ref, qkv_ref, lf_ref):
    d = x_ref.shape[1]
    x = x_ref[...]
    xn = (x * _rstd(x) * g_ref[...]).astype(BF16)
    f = _dot_nt(wf_ref[...], xn) + bf_ref[...]
    lf_ref[...] = _log_sigmoid(f)
    for part in range(3):
        cols = slice(part * d, (part + 1) * d)
        y = _dot(xn, w_ref[:, cols])
        if part == 0:
            y = y * (1.0 / math.sqrt(HEAD_DIM))
        qkv_ref[:, cols] = y.astype(BF16)


def _qkvf_proj(x, g, w_in, w_f_t, b_f, *, tm=512):
    t, d = x.shape
    n = 3 * d
    h = w_f_t.shape[0]
    const = lambda i: (0, 0)
    return pl.pallas_call(
        _qkvf_kernel,
        grid=(t // tm,),
        in_specs=[
            pl.BlockSpec((tm, d), lambda i: (i, 0)),
            _resident((1, d), const),
            _resident((d, n), const),
            _resident((h, d), const),
            _resident((h, 1), const),
        ],
        out_specs=[
            pl.BlockSpec((tm, n), lambda i: (i, 0)),
            pl.BlockSpec((h, tm), lambda i: (0, i)),
        ],
        out_shape=[
            jax.ShapeDtypeStruct((t, n), BF16),
            jax.ShapeDtypeStruct((h, t), F32),
        ],
        compiler_params=_params("parallel"),
        name="qkvf_proj",
    )(x, g, w_in, w_f_t, b_f)


def _cumsum_kernel(lf_ref, tri_ref, c_ref):
    h, s = lf_ref.shape
    tri = tri_ref[...]
    carry = jnp.zeros((h, 1), F32)
    for c0 in range(0, s, CUMSUM_CHUNK):
        hi, mid, lo = _split3(lf_ref[:, c0:c0 + CUMSUM_CHUNK])
        cs = _dot(lo, tri) + _dot(mid, tri) + _dot(hi, tri) + carry
        for hd in range(h):
            c_ref[hd, :, c0:c0 + CUMSUM_CHUNK] = cs[hd:hd + 1, :]
        carry = cs[:, CUMSUM_CHUNK - 1:CUMSUM_CHUNK]


def _forget_cumsum(lf_t, tri, *, seq):
    h, t = lf_t.shape
    return pl.pallas_call(
        _cumsum_kernel,
        grid=(t // seq,),
        in_specs=[
            pl.BlockSpec((h, seq), lambda b: (0, b)),
            pl.BlockSpec((CUMSUM_CHUNK, CUMSUM_CHUNK), lambda b: (0, 0)),
        ],
        out_specs=pl.BlockSpec((h, 1, seq), lambda b: (0, 0, b)),
        out_shape=jax.ShapeDtypeStruct((h, 1, t), F32),
        compiler_params=_params("parallel"),
        name="forget_cumsum",
    )(lf_t, tri)


def _head_masks(shape):
    lane = lax.broadcasted_iota(jnp.int32, shape, 1)
    return lane < HEAD_DIM


def _split_heads(x2, first_head_lanes, fill):
    return (jnp.where(first_head_lanes, x2, fill), jnp.where(first_head_lanes, fill, x2))


def _stack_heads(q2, first_head_lanes):
    return jnp.concatenate(_split_heads(q2, first_head_lanes, jnp.zeros_like(q2)), axis=0)


def _lane_tile(x, width):
    return jnp.concatenate([x] * (width // LANES), axis=1)


def _fox_kernel(q_ref, k_ref, v_ref, c_ref, o_ref, *scratch):
    tq = ATTN_TILE
    nq = q_ref.shape[0] // tq
    m_refs, acc_refs = scratch[:nq], scratch[nq:]
    first = _head_masks((tq, LANES))

    def tile(qs, m_ref, acc_ref, kt, diagonal):
        k = k_ref[kt * tq:(kt + 1) * tq, :]
        v = v_ref[kt * tq:(kt + 1) * tq, :]
        v_one = jnp.concatenate([v, jnp.ones_like(v)], axis=1)
        s = _dot_nt(qs, k)
        s = jnp.concatenate([s[hd * tq:(hd + 1) * tq] - c_ref[hd, :, kt * tq:(kt + 1) * tq]
                             for hd in range(HEADS_PER_BLOCK)], axis=0)
        if diagonal:
            row = lax.broadcasted_iota(jnp.int32, s.shape, 0) & (tq - 1)
            col = lax.broadcasted_iota(jnp.int32, s.shape, 1)
            s = jnp.where(col <= row, s, -jnp.inf)
        m_prev = m_ref[...]
        m_new = jnp.maximum(m_prev, jnp.max(s, axis=1, keepdims=True))
        alpha = jnp.exp(m_prev - m_new)
        p = jnp.exp(s - _lane_tile(m_new, tq))
        acc_ref[...] = _lane_tile(alpha, 2 * LANES) * acc_ref[...] + _dot(p.astype(BF16), v_one)
        m_ref[...] = m_new

    for qt in range(nq):
        qs = _stack_heads(q_ref[qt * tq:(qt + 1) * tq, :], first)
        m_ref, acc_ref = m_refs[qt], acc_refs[qt]
        m_ref[...] = jnp.full(m_ref.shape, -jnp.inf, F32)
        acc_ref[...] = jnp.zeros(acc_ref.shape, F32)
        for kt in range(qt + 1):
            tile(qs, m_ref, acc_ref, kt, kt == qt)
        out = acc_ref[:, :LANES] / acc_ref[:, LANES:]
        o_ref[qt * tq:(qt + 1) * tq, :] = jnp.where(first, out[:tq], out[tq:]).astype(BF16)


def _fox_attention(qkv, c, *, batch, seq, d_model):
    t = qkv.shape[0]
    n_pairs = d_model // LANES
    tq = ATTN_TILE
    nq = seq // tq
    assert tq & (tq - 1) == 0, "the diagonal mask takes row % tq as row & (tq - 1)"
    rows = HEADS_PER_BLOCK * tq
    return pl.pallas_call(
        _fox_kernel,
        grid=(batch, n_pairs),
        in_specs=[
            pl.BlockSpec((seq, LANES), lambda b, hp: (b, hp)),
            pl.BlockSpec((seq, LANES), lambda b, hp: (b, n_pairs + hp)),
            pl.BlockSpec((seq, LANES), lambda b, hp: (b, 2 * n_pairs + hp)),
            pl.BlockSpec((HEADS_PER_BLOCK, 1, seq), lambda b, hp: (hp, 0, b)),
        ],
        out_specs=pl.BlockSpec((seq, LANES), lambda b, hp: (b, hp)),
        out_shape=jax.ShapeDtypeStruct((t, d_model), BF16),
        scratch_shapes=([pltpu.VMEM((rows, LANES), F32)] * nq + [pltpu.VMEM((rows, 2 * LANES), F32)] * nq),
        compiler_params=_params("parallel", "parallel"),
        name="fox_attention",
    )(qkv, qkv, qkv, c)


def _sb_kernel(q_ref, k_ref, v_ref, tri_ref, o_ref, *scratch):
    tq = ATTN_TILE
    nq = q_ref.shape[0] // tq
    r_refs, acc_refs = scratch[:nq], scratch[nq:]
    sub = CUMSUM_CHUNK
    n_sub = tq // sub
    first = _head_masks((tq, LANES))

    def tile(qs, r_ref, acc_ref, kt, diagonal):
        k = k_ref[kt * tq:(kt + 1) * tq, :]
        v = v_ref[kt * tq:(kt + 1) * tq, :]
        z = _dot_nt(qs, k)
        nlm = jnp.maximum(z, 0.0) + jnp.log(1.0 + jnp.exp2(jnp.abs(z) * (-LOG2_E)))
        if diagonal:
            row = lax.broadcasted_iota(jnp.int32, z.shape, 0) & (tq - 1)
            col = lax.broadcasted_iota(jnp.int32, z.shape, 1)
            visible = col < row
            nlm = jnp.where(visible, nlm, 0.0)
        nlm16 = nlm.astype(BF16)
        p = [_dot(nlm16[:, sb * sub:(sb + 1) * sub], tri_ref[...]) for sb in range(n_sub)]
        r = r_ref[...]
        later = [None] * n_sub
        for sb in reversed(range(n_sub)):
            later[sb] = p[sb] + _lane_tile(r, sub)
            r = r + p[sb][:, 0:1]
        w = jnp.exp(z - jnp.concatenate(later, axis=1))
        if diagonal:
            w = jnp.where(visible, w, 0.0)
        acc_ref[...] += _dot(w.astype(BF16), v)
        r_ref[...] = r

    for qt in range(nq):
        qs = _stack_heads(q_ref[qt * tq:(qt + 1) * tq, :], first)
        r_ref, acc_ref = r_refs[qt], acc_refs[qt]
        r_ref[...] = jnp.zeros(r_ref.shape, F32)
        acc_ref[...] = jnp.zeros(acc_ref.shape, F32)
        for kt in reversed(range(qt + 1)):
            tile(qs, r_ref, acc_ref, kt, kt == qt)
        o_ref[qt * tq:(qt + 1) * tq, :] = jnp.where(first, acc_ref[:tq], acc_ref[tq:]).astype(BF16)


def _sb_attention(q, kv, tri, *, batch, seq, d_model):
    t = q.shape[0]
    n_pairs = d_model // LANES
    tq = ATTN_TILE
    nq = seq // tq
    assert tq & (tq - 1) == 0, "the diagonal mask takes row % tq as row & (tq - 1)"
    stat = pltpu.VMEM((HEADS_PER_BLOCK * tq, LANES), F32)
    return pl.pallas_call(
        _sb_kernel,
        grid=(batch, n_pairs),
        in_specs=[
            pl.BlockSpec((seq, LANES), lambda b, hp: (b, hp)),
            pl.BlockSpec((seq, LANES), lambda b, hp: (b, hp)),
            pl.BlockSpec((seq, LANES), lambda b, hp: (b, n_pairs + hp)),
            pl.BlockSpec(tri.shape, lambda b, hp: (0, 0)),
        ],
        out_specs=pl.BlockSpec((seq, LANES), lambda b, hp: (b, hp)),
        out_shape=jax.ShapeDtypeStruct((t, d_model), BF16),
        scratch_shapes=[stat] * (2 * nq),
        compiler_params=_params("parallel", "parallel"),
        name="sb_attention",
    )(q, kv, kv, tri)


def _oproj_router_kernel(o_ref, w_ref, h_ref, g_ref, rw_ref, rb_ref, h1_ref, fn_ref, idx_ref, gate_ref):
    h1 = h_ref[...] + _dot(o_ref[...], w_ref[...])
    h1_ref[...] = h1
    fn = h1 * _rstd(h1) * g_ref[...]
    fn_ref[...] = fn
    fn_hi = fn.astype(BF16)
    fn_lo = (fn - fn_hi.astype(F32)).astype(BF16)
    lhs = jnp.concatenate([fn_hi, fn_lo, fn_hi], axis=1)
    half = lhs.shape[0] // 2
    logits = jnp.concatenate([_dot(lhs[:half], rw_ref[...]), _dot(lhs[half:], rw_ref[...])], axis=0) + rb_ref[...]
    lane = lax.broadcasted_iota(jnp.int32, logits.shape, 1)
    logits = jnp.where(lane < N_EXPERTS, logits, -jnp.inf)
    lane_f = lane.astype(F32)
    m1 = jnp.max(logits, axis=1, keepdims=True)
    i1 = jnp.min(jnp.where(logits == m1, lane_f, float(LANES)), axis=1, keepdims=True)
    rest = jnp.where(lane_f == i1, -jnp.inf, logits)
    m2 = jnp.max(rest, axis=1, keepdims=True)
    i2 = jnp.min(jnp.where(rest == m2, lane_f, float(LANES)), axis=1, keepdims=True)
    e2 = jnp.exp(m2 - m1)
    g1 = 1.0 / (1.0 + e2)
    g2 = e2 / (1.0 + e2)
    idx_ref[...] = jnp.where(lane == 0, i1, jnp.where(lane == 1, i2, 0.0)).astype(jnp.int32)
    gate_ref[...] = jnp.where(lane == 0, g1, jnp.where(lane == 1, g2, 0.0))


def _oproj_router(o, w_o, h, g, rw, rb, *, tm=512):
    t, d = h.shape
    row = lambda i: (i, 0)
    const = lambda i: (0, 0)
    return pl.pallas_call(
        _oproj_router_kernel,
        grid=(t // tm,),
        in_specs=[pl.BlockSpec((tm, d), row), _resident((d, d), const), pl.BlockSpec((tm, d), row),
                  _resident((1, d), const), _resident(rw.shape, const), _resident((1, LANES), const)],
        out_specs=[pl.BlockSpec((tm, d), row), pl.BlockSpec((tm, d), row),
                   pl.BlockSpec((tm, LANES), row), pl.BlockSpec((tm, LANES), row)],
        out_shape=[jax.ShapeDtypeStruct((t, d), F32), jax.ShapeDtypeStruct((t, d), F32),
                   jax.ShapeDtypeStruct((t, LANES), jnp.int32), jax.ShapeDtypeStruct((t, LANES), F32)],
        compiler_params=_params("parallel"),
        name="oproj_norm_router",
    )(o, w_o, h, g, rw, rb)


def _swiglu_accumulate(x, w_gate_ref, w_up_ref, w_down_ref, out_ref, *, chunk, up_offset=0):
    f = w_down_ref.shape[0]
    for c0 in range(0, f, chunk):
        g = _dot(x, w_gate_ref[:, c0:c0 + chunk])
        u = _dot(x, w_up_ref[:, up_offset + c0:up_offset + c0 + chunk])
        act = (g * (1.0 / (1.0 + jnp.exp(-g))) * u).astype(BF16)
        out_ref[...] += _dot(act, w_down_ref[c0:c0 + chunk, :])


def _ple(h2, p, wp, wg):
    gate = 1.0 / (1.0 + jnp.exp(-_dot(h2.astype(BF16), wg)))
    return h2 + _dot(p.astype(BF16), wp) * gate


def _layer0_tail_kernel(o_ref, x_ref, p_ref, wo_ref, g_ref, wgu_ref, wd_ref, wp_ref, wg_ref, gains_ref,
                        wkv_ref, wq_ref, h3_ref, kv_ref, q_ref):
    h1 = x_ref[...] + _dot(o_ref[...], wo_ref[...])
    fn = (h1 * _rstd(h1) * g_ref[...]).astype(BF16)
    h3_ref[...] = h1
    _swiglu_accumulate(fn, wgu_ref, wgu_ref, wd_ref, h3_ref, chunk=DENSE_CHUNK, up_offset=wd_ref.shape[0])
    h3 = _ple(h3_ref[...], p_ref[...], wp_ref[...], wg_ref[...])
    h3_ref[...] = h3
    y = h3 * _rstd(h3)
    kv_ref[...] = _dot((y * gains_ref[0:1, :]).astype(BF16), wkv_ref[...]).astype(BF16)
    q = _dot((y * gains_ref[1:2, :]).astype(BF16), wq_ref[...])
    q_ref[...] = (q * (1.0 / math.sqrt(HEAD_DIM))).astype(BF16)


def _layer_rows(p, layer, tm):
    return pl.BlockSpec((None, tm, p.shape[2]), lambda i: (layer, i, 0))


def _layer0_tail(o, x, p, w_o, g, w_gu, w_down, wp, wg, gains, w_kv, w_q, *, tm=512):
    t, d = x.shape
    row = lambda i: (i, 0)
    const = lambda i: (0, 0)
    weights = (w_o, g, w_gu, w_down, wp, wg, gains, w_kv, w_q)
    return pl.pallas_call(
        _layer0_tail_kernel,
        grid=(t // tm,),
        in_specs=[pl.BlockSpec((tm, d), row), pl.BlockSpec((tm, d), row), _layer_rows(p, 0, tm)]
                 + [_resident(w.shape, const) for w in weights],
        out_specs=[pl.BlockSpec((tm, d), row), pl.BlockSpec((tm, w_kv.shape[1]), row), pl.BlockSpec((tm, d), row)],
        out_shape=[jax.ShapeDtypeStruct((t, d), F32), jax.ShapeDtypeStruct((t, w_kv.shape[1]), BF16),
                   jax.ShapeDtypeStruct((t, d), BF16)],
        compiler_params=_params("parallel"),
        name="layer0_tail",
    )(o, x, p, *weights)


def _dispatch_kernel(pend_ref, padded_ref, dest_ref, x_ref, xb_hbm, zero_ref, sem, zero_sem, *, tm):
    n_tiles = xb_hbm.shape[0] // EXPERT_TILE

    def zero_tile(tile):
        row0 = pl.multiple_of(tile * EXPERT_TILE, EXPERT_TILE)
        return pltpu.make_async_copy(zero_ref, xb_hbm.at[pl.ds(row0, EXPERT_TILE)], zero_sem)

    def last_tile(e):
        return pend_ref[e] // EXPERT_TILE - 1

    def for_unused_tiles(fn):
        def body(tile, carry):
            fn(tile)
            return carry
        lax.fori_loop(pend_ref[N_EXPERTS - 1] // EXPERT_TILE, n_tiles, body, 0)

    @pl.when(pl.program_id(0) == 0)
    def _():
        zero_ref[...] = jnp.zeros(zero_ref.shape, zero_ref.dtype)
        for e in range(N_EXPERTS):
            pl.when(padded_ref[e] > 0)(lambda e=e: zero_tile(last_tile(e)).start())
        for_unused_tiles(lambda tile: zero_tile(tile).start())
        for e in range(N_EXPERTS):
            pl.when(padded_ref[e] > 0)(lambda e=e: zero_tile(last_tile(e)).wait())
        for_unused_tiles(lambda tile: zero_tile(tile).wait())

    def issue(r, carry):
        for k in range(TOP_K):
            pltpu.make_async_copy(x_ref.at[pl.ds(r, 1)],
                                  xb_hbm.at[pl.ds(dest_ref[0, 0, r * TOP_K + k], 1)], sem).start()
        return carry

    lax.fori_loop(0, tm, issue, 0, unroll=8)
    for _ in range(TOP_K):
        pltpu.make_async_copy(x_ref, xb_hbm.at[pl.ds(0, tm)], sem).wait()


def _dispatch(x, dest3, pend, padded, n_rows, *, tm):
    t, d = x.shape
    return pl.pallas_call(
        functools.partial(_dispatch_kernel, tm=tm),
        grid_spec=pltpu.PrefetchScalarGridSpec(
            num_scalar_prefetch=2,
            grid=(t // tm,),
            in_specs=[
                pl.BlockSpec((1, 1, tm * TOP_K), lambda i, pe, pa: (i, 0, 0), memory_space=pltpu.SMEM),
                pl.BlockSpec((tm, d), lambda i, pe, pa: (i, 0)),
            ],
            out_specs=pl.BlockSpec(memory_space=pl.ANY),
            scratch_shapes=[pltpu.VMEM((EXPERT_TILE, d), x.dtype), pltpu.SemaphoreType.DMA,
                            pltpu.SemaphoreType.DMA],
        ),
        out_shape=jax.ShapeDtypeStruct((n_rows, d), x.dtype),
        compiler_params=_params("arbitrary"),
        name="moe_dispatch",
    )(pend, padded, dest3, x)


def _expert_kernel(te_ref, tv_ref, x_ref, wg_ref, wu_ref, wd_ref, y_ref, *, chunk):
    del te_ref
    y_ref[...] = jnp.zeros(y_ref.shape, F32)

    @pl.when(tv_ref[pl.program_id(0)] > 0)
    def _():
        _swiglu_accumulate(x_ref[...].astype(BF16), wg_ref.at[0], wu_ref.at[0], wd_ref.at[0], y_ref, chunk=chunk)


def _experts(xb, w_gu, w_down, tile_expert, tile_valid, *, chunk=EXPERT_CHUNK):
    n_rows, d = xb.shape
    fe = w_down.shape[1]
    n_tiles = n_rows // EXPERT_TILE
    return pl.pallas_call(
        functools.partial(_expert_kernel, chunk=chunk),
        grid_spec=pltpu.PrefetchScalarGridSpec(
            num_scalar_prefetch=2,
            grid=(n_tiles,),
            in_specs=[
                pl.BlockSpec((EXPERT_TILE, d), lambda ti, te, tv: (ti, 0)),
                pl.BlockSpec((1, d, fe), lambda ti, te, tv: (te[ti], 0, 0)),
                pl.BlockSpec((1, d, fe), lambda ti, te, tv: (te[ti], 0, 1)),
                _resident((1, fe, d), lambda ti, te, tv: (te[ti], 0, 0)),
            ],
            out_specs=pl.BlockSpec((EXPERT_TILE, d), lambda ti, te, tv: (ti, 0)),
        ),
        out_shape=jax.ShapeDtypeStruct((n_rows, d), F32),
        compiler_params=_params("arbitrary"),
        name="moe_experts",
    )(tile_expert, tile_valid, xb, w_gu, w_gu, w_down)


def _combine_kernel(dest_ref, dest_next_ref, yb_hbm, h_ref, gate_ref, p_ref, wp_ref, wg_ref, g_ref, o_ref,
                    y_ref, sems, *, tm):
    i = pl.program_id(0)
    slot = i % 2

    def gather(d_ref, s):
        def issue(r, carry):
            for k in range(TOP_K):
                pltpu.make_async_copy(yb_hbm.at[pl.ds(d_ref[0, 0, r * TOP_K + k], 1)],
                                      y_ref.at[s, k, pl.ds(r, 1)], sems.at[s]).start()
            return carry
        lax.fori_loop(0, tm, issue, 0, unroll=8)

    pl.when(i == 0)(lambda: gather(dest_ref, 0))
    pl.when(i + 1 < pl.num_programs(0))(lambda: gather(dest_next_ref, 1 - slot))
    for k in range(TOP_K):
        pltpu.make_async_copy(yb_hbm.at[pl.ds(0, tm)], y_ref.at[slot, k], sems.at[slot]).wait()

    gates = gate_ref[...]
    h2 = h_ref[...] + (y_ref[slot, 0] * gates[:, 0:1] + y_ref[slot, 1] * gates[:, 1:2])
    h3 = _ple(h2, p_ref[...], wp_ref[...], wg_ref[...])
    o_ref[...] = h3 * _rstd(h3) * g_ref[...]


def _combine(dest3, yb, h, gates, p, wp, wg, g, *, tm):
    t, d = h.shape
    dp = p.shape[2]
    n = t // tm
    row = lambda i: (i, 0)
    const = lambda i: (0, 0)
    dest_spec = lambda index_map: pl.BlockSpec((1, 1, tm * TOP_K), index_map, memory_space=pltpu.SMEM)
    return pl.pallas_call(
        functools.partial(_combine_kernel, tm=tm),
        grid=(n,),
        in_specs=[
            dest_spec(lambda i: (i, 0, 0)),
            dest_spec(lambda i: (jnp.minimum(i + 1, n - 1), 0, 0)),
            pl.BlockSpec(memory_space=pl.ANY),
            pl.BlockSpec((tm, d), row),
            pl.BlockSpec((tm, LANES), row),
            _layer_rows(p, 1, tm),
            _resident((dp, d), const),
            _resident((d, d), const),
            _resident((1, d), const),
        ],
        out_specs=pl.BlockSpec((tm, d), row),
        out_shape=jax.ShapeDtypeStruct((t, d), F32),
        scratch_shapes=[pltpu.VMEM((2, TOP_K, tm, d), F32), pltpu.SemaphoreType.DMA((2,))],
        compiler_params=_params("arbitrary"),
        name="moe_combine",
    )(dest3, dest3, yb, h, gates, p, wp, wg, g)


def _routing_tables(idx, n_tokens, *, tm):
    e_flat = idx.reshape(-1)
    onehot = (e_flat[:, None] == jnp.arange(N_EXPERTS, dtype=jnp.int32)[None, :]).astype(jnp.int32)
    csum = jnp.cumsum(onehot, axis=0)
    counts = csum[-1]
    padded = (counts + EXPERT_TILE - 1) // EXPERT_TILE * EXPERT_TILE
    pend = jnp.cumsum(padded)
    pstart = pend - padded
    dest = jnp.sum(onehot * (csum - 1 + pstart[None, :]), axis=1)
    n_tiles = (n_tokens * TOP_K) // EXPERT_TILE + N_EXPERTS
    tile_row0 = jnp.arange(n_tiles, dtype=jnp.int32) * EXPERT_TILE
    tile_expert = jnp.minimum(jnp.sum((tile_row0[:, None] >= pend[None, :]).astype(jnp.int32), axis=1),
                              N_EXPERTS - 1)
    tile_valid = (tile_row0 < pend[-1]).astype(jnp.int32)
    dest3 = dest.astype(jnp.int32).reshape(n_tokens // tm, 1, tm * TOP_K)
    return dest3, tile_expert, tile_valid, pend.astype(jnp.int32), padded.astype(jnp.int32), n_tiles


def kernel(x, p, attn_norm, ffn_norm, w_in_a, b_f, w_o_a, kv_norm, w_kv, w_q_b, w_o_b, w_gu_dense, w_down_dense, router_w, router_b, w_gu_moe, w_down_moe, w_ple_proj, w_ple_gate, final_norm):
    batch, seq, d = x.shape
    t = batch * seq
    n_heads = d // HEAD_DIM
    x2 = x.reshape(t, d)
    p2 = p.reshape(p.shape[0], t, p.shape[-1])
    bf = lambda w: w.astype(BF16)

    ii = lax.broadcasted_iota(jnp.int32, (CUMSUM_CHUNK, CUMSUM_CHUNK), 0)
    jj = lax.broadcasted_iota(jnp.int32, (CUMSUM_CHUNK, CUMSUM_CHUNK), 1)
    tri_le = (ii <= jj).astype(BF16)
    tri_ge = (ii >= jj).astype(BF16)

    w_in = bf(w_in_a[0])
    qkv, lf_t = _qkvf_proj(x2, attn_norm[0:1], w_in, w_in[:, 3 * d:].T, b_f[0].reshape(n_heads, 1))
    c = _forget_cumsum(lf_t, tri_le, seq=seq)
    o0 = _fox_attention(qkv, c, batch=batch, seq=seq, d_model=d)
    gains = jnp.stack([kv_norm, attn_norm[1]])
    h3, kv, q1 = _layer0_tail(o0, x2, p2, bf(w_o_a[0]), ffn_norm[0:1], bf(w_gu_dense[0]), bf(w_down_dense[0]),
                              bf(w_ple_proj[0]), bf(w_ple_gate[0]), gains, bf(w_kv), bf(w_q_b[0]))

    o1 = _sb_attention(q1, kv, tri_ge, batch=batch, seq=seq, d_model=d)
    rw = jnp.pad(router_w[0], ((0, 0), (0, LANES - N_EXPERTS)))
    rw_hi = bf(rw)
    rw_lo = bf(rw - rw_hi.astype(F32))
    rw = jnp.concatenate([rw_hi, rw_hi, rw_lo], axis=0)
    rb = jnp.pad(router_b[0], (0, LANES - N_EXPERTS)).reshape(1, LANES)
    h4, fn1, idx, gates = _oproj_router(o1, bf(w_o_b[0]), h3, ffn_norm[1:2], rw, rb)

    tm = 512
    dest3, tile_expert, tile_valid, pend, padded, n_tiles = _routing_tables(idx[:, :TOP_K], t, tm=tm)
    xb = _dispatch(fn1, dest3, pend, padded, n_tiles * EXPERT_TILE, tm=tm)
    yb = _experts(xb, bf(w_gu_moe[0]), bf(w_down_moe[0]), tile_expert, tile_valid)
    out = _combine(dest3, yb, h4, gates, p2, bf(w_ple_proj[1]), bf(w_ple_gate[1]),
                   final_norm.reshape(1, d), tm=tm)
    return out.reshape(batch, seq, d)
```

```python
import functools
import math

import jax
import jax.numpy as jnp
from jax import lax
from jax.experimental import pallas as pl
from jax.experimental.pallas import tpu as pltpu

HEAD_DIM = 64
LANES = 128
HEADS_PER_BLOCK = LANES // HEAD_DIM
N_EXPERTS = 8
TOP_K = 2
RMS_EPS = 1e-6
LOG2_E = 1.4426950408889634
ATTN_TILE = 256
CUMSUM_CHUNK = 256
EXPERT_TILE = 512
DENSE_CHUNK = 256
EXPERT_CHUNK = 512
VMEM_LIMIT = 56 * 1024 * 1024

BF16 = jnp.bfloat16
F32 = jnp.float32


def _params(*semantics):
    return pltpu.CompilerParams(dimension_semantics=semantics, vmem_limit_bytes=VMEM_LIMIT)


def _dot(a, b):
    return jnp.dot(a, b, preferred_element_type=F32)


def _dot_nt(a, b):
    return lax.dot_general(a, b, (((1,), (1,)), ((), ())), preferred_element_type=F32)


def _rstd(x):
    return lax.rsqrt(jnp.mean(x * x, axis=-1, keepdims=True) + RMS_EPS)


def _log_sigmoid(u):
    return jnp.minimum(u, 0.0) - jnp.log1p(jnp.exp(-jnp.abs(u)))


def _split3(x):
    hi = x.astype(BF16)
    r1 = x - hi.astype(F32)
    mid = r1.astype(BF16)
    lo = (r1 - mid.astype(F32)).astype(BF16)
    return hi, mid, lo


def _resident(block_shape, index_map):
    return pl.BlockSpec(block_shape, index_map, pipeline_mode=pl.Buffered(1))


def _qkvf_kernel(x_ref, g_ref, w_ref, wf_ref, bf_ref, qkv_ref, lf_ref):
    d = x_ref.shape[1]
    x = x_ref[...]
    xn = (x * _rstd(x) * g_ref[...]).astype(BF16)
    f = _dot_nt(wf_ref[...], xn) + bf_ref[...]
    lf_ref[...] = _log_sigmoid(f)
    for part in range(3):
        cols = slice(part * d, (part + 1) * d)
        y = _dot(xn, w_ref[:, cols])
        if part == 0:
            y = y * (1.0 / math.sqrt(HEAD_DIM))
        qkv_ref[:, cols] = y.astype(BF16)


def _qkvf_proj(x, g, w_in, w_f_t, b_f, *, tm=512):
    t, d = x.shape
    n = 3 * d
    h = w_f_t.shape[0]
    const = lambda i: (0, 0)
    return pl.pallas_call(
        _qkvf_kernel,
        grid=(t // tm,),
        in_specs=[
            pl.BlockSpec((tm, d), lambda i: (i, 0)),
            _resident((1, d), const),
            _resident((d, n), const),
            _resident((h, d), const),
            _resident((h, 1), const),
        ],
        out_specs=[
            pl.BlockSpec((tm, n), lambda i: (i, 0)),
            pl.BlockSpec((h, tm), lambda i: (0, i)),
        ],
        out_shape=[
            jax.ShapeDtypeStruct((t, n), BF16),
            jax.ShapeDtypeStruct((h, t), F32),
        ],
        compiler_params=_params("parallel"),
        name="qkvf_proj",
    )(x, g, w_in, w_f_t, b_f)


def _cumsum_kernel(lf_ref, tri_ref, c_ref):
    h, s = lf_ref.shape
    tri = tri_ref[...]
    carry = jnp.zeros((h, 1), F32)
    for c0 in range(0, s, CUMSUM_CHUNK):
        hi, mid, lo = _split3(lf_ref[:, c0:c0 + CUMSUM_CHUNK])
        cs = _dot(lo, tri) + _dot(mid, tri) + _dot(hi, tri) + carry
        for hd in range(h):
            c_ref[hd, :, c0:c0 + CUMSUM_CHUNK] = cs[hd:hd + 1, :]
        carry = cs[:, CUMSUM_CHUNK - 1:CUMSUM_CHUNK]


def _forget_cumsum(lf_t, tri, *, seq):
    h, t = lf_t.shape
    return pl.pallas_call(
        _cumsum_kernel,
        grid=(t // seq,),
        in_specs=[
            pl.BlockSpec((h, seq), lambda b: (0, b)),
            pl.BlockSpec((CUMSUM_CHUNK, CUMSUM_CHUNK), lambda b: (0, 0)),
        ],
        out_specs=pl.BlockSpec((h, 1, seq), lambda b: (0, 0, b)),
        out_shape=jax.ShapeDtypeStruct((h, 1, t), F32),
        compiler_params=_params("parallel"),
        name="forget_cumsum",
    )(lf_t, tri)


def _head_masks(shape):
    lane = lax.broadcasted_iota(jnp.int32, shape, 1)
    return lane < HEAD_DIM


def _split_heads(x2, first_head_lanes, fill):
    return (jnp.where(first_head_lanes, x2, fill), jnp.where(first_head_lanes, fill, x2))


def _stack_heads(q2, first_head_lanes):
    return jnp.concatenate(_split_heads(q2, first_head_lanes, jnp.zeros_like(q2)), axis=0)


def _lane_tile(x, width):
    return jnp.concatenate([x] * (width // LANES), axis=1)


def _fox_kernel(q_ref, k_ref, v_ref, c_ref, o_ref, *scratch):
    tq = ATTN_TILE
    nq = q_ref.shape[0] // tq
    m_refs, acc_refs = scratch[:nq], scratch[nq:]
    first = _head_masks((tq, LANES))

    def tile(qs, m_ref, acc_ref, kt, diagonal):
        k = k_ref[kt * tq:(kt + 1) * tq, :]
        v = v_ref[kt * tq:(kt + 1) * tq, :]
        v_one = jnp.concatenate([v, jnp.ones_like(v)], axis=1)
        s = _dot_nt(qs, k)
        s = jnp.concatenate([s[hd * tq:(hd + 1) * tq] - c_ref[hd, :, kt * tq:(kt + 1) * tq]
                             for hd in range(HEADS_PER_BLOCK)], axis=0)
        if diagonal:
            row = lax.broadcasted_iota(jnp.int32, s.shape, 0) & (tq - 1)
            col = lax.broadcasted_iota(jnp.int32, s.shape, 1)
            s = jnp.where(col <= row, s, -jnp.inf)
        m_prev = m_ref[...]
        m_new = jnp.maximum(m_prev, jnp.max(s, axis=1, keepdims=True))
        alpha = jnp.exp(m_prev - m_new)
        p = jnp.exp(s - _lane_tile(m_new, tq))
        acc_ref[...] = _lane_tile(alpha, 2 * LANES) * acc_ref[...] + _dot(p.astype(BF16), v_one)
        m_ref[...] = m_new

    for qt in range(nq):
        qs = _stack_heads(q_ref[qt * tq:(qt + 1) * tq, :], first)
        m_ref, acc_ref = m_refs[qt], acc_refs[qt]
        m_ref[...] = jnp.full(m_ref.shape, -jnp.inf, F32)
        acc_ref[...] = jnp.zeros(acc_ref.shape, F32)
        for kt in range(qt + 1):
            tile(qs, m_ref, acc_ref, kt, kt == qt)
        out = acc_ref[:, :LANES] / acc_ref[:, LANES:]
        o_ref[qt * tq:(qt + 1) * tq, :] = jnp.where(first, out[:tq], out[tq:]).astype(BF16)


def _fox_attention(qkv, c, *, batch, seq, d_model):
    t = qkv.shape[0]
    n_pairs = d_model // LANES
    tq = ATTN_TILE
    nq = seq // tq
    assert tq & (tq - 1) == 0, "the diagonal mask takes row % tq as row & (tq - 1)"
    rows = HEADS_PER_BLOCK * tq
    return pl.pallas_call(
        _fox_kernel,
        grid=(batch, n_pairs),
        in_specs=[
            pl.BlockSpec((seq, LANES), lambda b, hp: (b, hp)),
            pl.BlockSpec((seq, LANES), lambda b, hp: (b, n_pairs + hp)),
            pl.BlockSpec((seq, LANES), lambda b, hp: (b, 2 * n_pairs + hp)),
            pl.BlockSpec((HEADS_PER_BLOCK, 1, seq), lambda b, hp: (hp, 0, b)),
        ],
        out_specs=pl.BlockSpec((seq, LANES), lambda b, hp: (b, hp)),
        out_shape=jax.ShapeDtypeStruct((t, d_model), BF16),
        scratch_shapes=([pltpu.VMEM((rows, LANES), F32)] * nq + [pltpu.VMEM((rows, 2 * LANES), F32)] * nq),
        compiler_params=_params("parallel", "parallel"),
        name="fox_attention",
    )(qkv, qkv, qkv, c)


def _sb_kernel(q_ref, k_ref, v_ref, tri_ref, o_ref, *scratch):
    tq = ATTN_TILE
    nq = q_ref.shape[0] // tq
    r_refs, acc_refs = scratch[:nq], scratch[nq:]
    sub = CUMSUM_CHUNK
    n_sub = tq // sub
    first = _head_masks((tq, LANES))

    def tile(qs, r_ref, acc_ref, kt, diagonal):
        k = k_ref[kt * tq:(kt + 1) * tq, :]
        v = v_ref[kt * tq:(kt + 1) * tq, :]
        z = _dot_nt(qs, k)
        nlm = jnp.maximum(z, 0.0) + jnp.log(1.0 + jnp.exp2(jnp.abs(z) * (-LOG2_E)))
        if diagonal:
            row = lax.broadcasted_iota(jnp.int32, z.shape, 0) & (tq - 1)
            col = lax.broadcasted_iota(jnp.int32, z.shape, 1)
            visible = col < row
            nlm = jnp.where(visible, nlm, 0.0)
        nlm16 = nlm.astype(BF16)
        p = [_dot(nlm16[:, sb * sub:(sb + 1) * sub], tri_ref[...]) for sb in range(n_sub)]
        r = r_ref[...]
        later = [None] * n_sub
        for sb in reversed(range(n_sub)):
            later[sb] = p[sb] + _lane_tile(r, sub)
            r = r + p[sb][:, 0:1]
        w = jnp.exp(z - jnp.concatenate(later, axis=1))
        if diagonal:
            w = jnp.where(visible, w, 0.0)
        acc_ref[...] += _dot(w.astype(BF16), v)
        r_ref[...] = r

    for qt in range(nq):
        qs = _stack_heads(q_ref[qt * tq:(qt + 1) * tq, :], first)
        r_ref, acc_ref = r_refs[qt], acc_refs[qt]
        r_ref[...] = jnp.zeros(r_ref.shape, F32)
        acc_ref[...] = jnp.zeros(acc_ref.shape, F32)
        for kt in reversed(range(qt + 1)):
            tile(qs, r_ref, acc_ref, kt, kt == qt)
        o_ref[qt * tq:(qt + 1) * tq, :] = jnp.where(first, acc_ref[:tq], acc_ref[tq:]).astype(BF16)


def _sb_attention(q, kv, tri, *, batch, seq, d_model):
    t = q.shape[0]
    n_pairs = d_model // LANES
    tq = ATTN_TILE
    nq = seq // tq
    assert tq & (tq - 1) == 0, "the diagonal mask takes row % tq as row & (tq - 1)"
    stat = pltpu.VMEM((HEADS_PER_BLOCK * tq, LANES), F32)
    return pl.pallas_call(
        _sb_kernel,
        grid=(batch, n_pairs),
        in_specs=[
            pl.BlockSpec((seq, LANES), lambda b, hp: (b, hp)),
            pl.BlockSpec((seq, LANES), lambda b, hp: (b, hp)),
            pl.BlockSpec((seq, LANES), lambda b, hp: (b, n_pairs + hp)),
            pl.BlockSpec(tri.shape, lambda b, hp: (0, 0)),
        ],
        out_specs=pl.BlockSpec((seq, LANES), lambda b, hp: (b, hp)),
        out_shape=jax.ShapeDtypeStruct((t, d_model), BF16),
        scratch_shapes=[stat] * (2 * nq),
        compiler_params=_params("parallel", "parallel"),
        name="sb_attention",
    )(q, kv, kv, tri)


def _oproj_router_kernel(o_ref, w_ref, h_ref, g_ref, rw_ref, rb_ref, h1_ref, fn_ref, idx_ref, gate_ref):
    h1 = h_ref[...] + _dot(o_ref[...], w_ref[...])
    h1_ref[...] = h1
    fn = h1 * _rstd(h1) * g_ref[...]
    fn_ref[...] = fn
    fn_hi = fn.astype(BF16)
    fn_lo = (fn - fn_hi.astype(F32)).astype(BF16)
    lhs = jnp.concatenate([fn_hi, fn_lo, fn_hi], axis=1)
    half = lhs.shape[0] // 2
    logits = jnp.concatenate([_dot(lhs[:half], rw_ref[...]), _dot(lhs[half:], rw_ref[...])], axis=0) + rb_ref[...]
    lane = lax.broadcasted_iota(jnp.int32, logits.shape, 1)
    logits = jnp.where(lane < N_EXPERTS, logits, -jnp.inf)
    lane_f = lane.astype(F32)
    m1 = jnp.max(logits, axis=1, keepdims=True)
    i1 = jnp.min(jnp.where(logits == m1, lane_f, float(LANES)), axis=1, keepdims=True)
    rest = jnp.where(lane_f == i1, -jnp.inf, logits)
    m2 = jnp.max(rest, axis=1, keepdims=True)
    i2 = jnp.min(jnp.where(rest == m2, lane_f, float(LANES)), axis=1, keepdims=True)
    e2 = jnp.exp(m2 - m1)
    g1 = 1.0 / (1.0 + e2)
    g2 = e2 / (1.0 + e2)
    idx_ref[...] = jnp.where(lane == 0, i1, jnp.where(lane == 1, i2, 0.0)).astype(jnp.int32)
    gate_ref[...] = jnp.where(lane == 0, g1, jnp.where(lane == 1, g2, 0.0))


def _oproj_router(o, w_o, h, g, rw, rb, *, tm=512):
    t, d = h.shape
    row = lambda i: (i, 0)
    const = lambda i: (0, 0)
    return pl.pallas_call(
        _oproj_router_kernel,
        grid=(t // tm,),
        in_specs=[pl.BlockSpec((tm, d), row), _resident((d, d), const), pl.BlockSpec((tm, d), row),
                  _resident((1, d), const), _resident(rw.shape, const), _resident((1, LANES), const)],
        out_specs=[pl.BlockSpec((tm, d), row), pl.BlockSpec((tm, d), row),
                   pl.BlockSpec((tm, LANES), row), pl.BlockSpec((tm, LANES), row)],
        out_shape=[jax.ShapeDtypeStruct((t, d), F32), jax.ShapeDtypeStruct((t, d), F32),
                   jax.ShapeDtypeStruct((t, LANES), jnp.int32), jax.ShapeDtypeStruct((t, LANES), F32)],
        compiler_params=_params("parallel"),
        name="oproj_norm_router",
    )(o, w_o, h, g, rw, rb)


def _swiglu_accumulate(x, w_gate_ref, w_up_ref, w_down_ref, out_ref, *, chunk, up_offset=0):
    f = w_down_ref.shape[0]
    for c0 in range(0, f, chunk):
        g = _dot(x, w_gate_ref[:, c0:c0 + chunk])
        u = _dot(x, w_up_ref[:, up_offset + c0:up_offset + c0 + chunk])
        act = (g * (1.0 / (1.0 + jnp.exp(-g))) * u).astype(BF16)
        out_ref[...] += _dot(act, w_down_ref[c0:c0 + chunk, :])


def _ple(h2, p, wp, wg):
    gate = 1.0 / (1.0 + jnp.exp(-_dot(h2.astype(BF16), wg)))
    return h2 + _dot(p.astype(BF16), wp) * gate


def _layer0_tail_kernel(o_ref, x_ref, p_ref, wo_ref, g_ref, wgu_ref, wd_ref, wp_ref, wg_ref, gains_ref,
                        wkv_ref, wq_ref, h3_ref, kv_ref, q_ref):
    h1 = x_ref[...] + _dot(o_ref[...], wo_ref[...])
    fn = (h1 * _rstd(h1) * g_ref[...]).astype(BF16)
    h3_ref[...] = h1
    _swiglu_accumulate(fn, wgu_ref, wgu_ref, wd_ref, h3_ref, chunk=DENSE_CHUNK, up_offset=wd_ref.shape[0])
    h3 = _ple(h3_ref[...], p_ref[...], wp_ref[...], wg_ref[...])
    h3_ref[...] = h3
    y = h3 * _rstd(h3)
    kv_ref[...] = _dot((y * gains_ref[0:1, :]).astype(BF16), wkv_ref[...]).astype(BF16)
    q = _dot((y * gains_ref[1:2, :]).astype(BF16), wq_ref[...])
    q_ref[...] = (q * (1.0 / math.sqrt(HEAD_DIM))).astype(BF16)


def _layer_rows(p, layer, tm):
    return pl.BlockSpec((None, tm, p.shape[2]), lambda i: (layer, i, 0))


def _layer0_tail(o, x, p, w_o, g, w_gu, w_down, wp, wg, gains, w_kv, w_q, *, tm=512):
    t, d = x.shape
    row = lambda i: (i, 0)
    const = lambda i: (0, 0)
    weights = (w_o, g, w_gu, w_down, wp, wg, gains, w_kv, w_q)
    return pl.pallas_call(
        _layer0_tail_kernel,
        grid=(t // tm,),
        in_specs=[pl.BlockSpec((tm, d), row), pl.BlockSpec((tm, d), row), _layer_rows(p, 0, tm)]
                 + [_resident(w.shape, const) for w in weights],
        out_specs=[pl.BlockSpec((tm, d), row), pl.BlockSpec((tm, w_kv.shape[1]), row), pl.BlockSpec((tm, d), row)],
        out_shape=[jax.ShapeDtypeStruct((t, d), F32), jax.ShapeDtypeStruct((t, w_kv.shape[1]), BF16),
                   jax.ShapeDtypeStruct((t, d), BF16)],
        compiler_params=_params("parallel"),
        name="layer0_tail",
    )(o, x, p, *weights)


def _dispatch_kernel(pend_ref, padded_ref, dest_ref, x_ref, xb_hbm, zero_ref, sem, zero_sem, *, tm):
    n_tiles = xb_hbm.shape[0] // EXPERT_TILE

    def zero_tile(tile):
        row0 = pl.multiple_of(tile * EXPERT_TILE, EXPERT_TILE)
        return pltpu.make_async_copy(zero_ref, xb_hbm.at[pl.ds(row0, EXPERT_TILE)], zero_sem)

    def last_tile(e):
        return pend_ref[e] // EXPERT_TILE - 1

    def for_unused_tiles(fn):
        def body(tile, carry):
            fn(tile)
            return carry
        lax.fori_loop(pend_ref[N_EXPERTS - 1] // EXPERT_TILE, n_tiles, body, 0)

    @pl.when(pl.program_id(0) == 0)
    def _():
        zero_ref[...] = jnp.zeros(zero_ref.shape, zero_ref.dtype)
        for e in range(N_EXPERTS):
            pl.when(padded_ref[e] > 0)(lambda e=e: zero_tile(last_tile(e)).start())
        for_unused_tiles(lambda tile: zero_tile(tile).start())
        for e in range(N_EXPERTS):
            pl.when(padded_ref[e] > 0)(lambda e=e: zero_tile(last_tile(e)).wait())
        for_unused_tiles(lambda tile: zero_tile(tile).wait())

    def issue(r, carry):
        for k in range(TOP_K):
            pltpu.make_async_copy(x_ref.at[pl.ds(r, 1)],
                                  xb_hbm.at[pl.ds(dest_ref[0, 0, r * TOP_K + k], 1)], sem).start(priority=k)
        return carry

    lax.fori_loop(0, tm, issue, 0, unroll=8)
    for _ in range(TOP_K):
        pltpu.make_async_copy(x_ref, xb_hbm.at[pl.ds(0, tm)], sem).wait()


def _dispatch(x, dest3, pend, padded, n_rows, *, tm):
    t, d = x.shape
    return pl.pallas_call(
        functools.partial(_dispatch_kernel, tm=tm),
        grid_spec=pltpu.PrefetchScalarGridSpec(
            num_scalar_prefetch=2,
            grid=(t // tm,),
            in_specs=[
                pl.BlockSpec((1, 1, tm * TOP_K), lambda i, pe, pa: (i, 0, 0), memory_space=pltpu.SMEM),
                pl.BlockSpec((tm, d), lambda i, pe, pa: (i, 0)),
            ],
            out_specs=pl.BlockSpec(memory_space=pl.ANY),
            scratch_shapes=[pltpu.VMEM((EXPERT_TILE, d), x.dtype), pltpu.SemaphoreType.DMA,
                            pltpu.SemaphoreType.DMA],
        ),
        out_shape=jax.ShapeDtypeStruct((n_rows, d), x.dtype),
        compiler_params=_params("arbitrary"),
        name="moe_dispatch",
    )(pend, padded, dest3, x)


def _expert_kernel(te_ref, th_ref, x_ref, wg_ref, wu_ref, wd_ref, y_ref, *, chunk):
    del te_ref
    halves = th_ref[pl.program_id(0)]
    half = EXPERT_TILE // 2
    y_ref[...] = jnp.zeros(y_ref.shape, F32)

    @pl.when(halves == 2)
    def _():
        _swiglu_accumulate(x_ref[...].astype(BF16), wg_ref.at[0], wu_ref.at[0], wd_ref.at[0], y_ref, chunk=chunk)

    @pl.when(halves == 1)
    def _():
        _swiglu_accumulate(x_ref[:half, :].astype(BF16), wg_ref.at[0], wu_ref.at[0], wd_ref.at[0],
                           y_ref.at[pl.ds(0, half)], chunk=chunk)


def _experts(xb, w_gu, w_down, tile_expert, tile_halves, *, chunk=EXPERT_CHUNK):
    n_rows, d = xb.shape
    fe = w_down.shape[1]
    n_tiles = n_rows // EXPERT_TILE
    return pl.pallas_call(
        functools.partial(_expert_kernel, chunk=chunk),
        grid_spec=pltpu.PrefetchScalarGridSpec(
            num_scalar_prefetch=2,
            grid=(n_tiles,),
            in_specs=[
                pl.BlockSpec((EXPERT_TILE, d), lambda ti, te, tv: (ti, 0)),
                pl.BlockSpec((1, d, fe), lambda ti, te, tv: (te[ti], 0, 0)),
                pl.BlockSpec((1, d, fe), lambda ti, te, tv: (te[ti], 0, 1)),
                _resident((1, fe, d), lambda ti, te, tv: (te[ti], 0, 0)),
            ],
            out_specs=pl.BlockSpec((EXPERT_TILE, d), lambda ti, te, tv: (ti, 0)),
        ),
        out_shape=jax.ShapeDtypeStruct((n_rows, d), F32),
        compiler_params=_params("arbitrary"),
        name="moe_experts",
    )(tile_expert, tile_halves, xb, w_gu, w_gu, w_down)


def _combine_kernel(dest_ref, dest_next_ref, yb_hbm, h_ref, gate_ref, p_ref, wp_ref, wg_ref, g_ref, o_ref,
                    y_ref, sems, *, tm):
    i = pl.program_id(0)
    slot = i % 2

    def gather(d_ref, s):
        def issue(r, carry):
            for k in range(TOP_K):
                pltpu.make_async_copy(yb_hbm.at[pl.ds(d_ref[0, 0, r * TOP_K + k], 1)],
                                      y_ref.at[s, k, pl.ds(r, 1)], sems.at[s]).start(priority=k)
            return carry
        lax.fori_loop(0, tm, issue, 0, unroll=8)

    pl.when(i == 0)(lambda: gather(dest_ref, 0))
    pl.when(i + 1 < pl.num_programs(0))(lambda: gather(dest_next_ref, 1 - slot))
    for k in range(TOP_K):
        pltpu.make_async_copy(yb_hbm.at[pl.ds(0, tm)], y_ref.at[slot, k], sems.at[slot]).wait()

    gates = gate_ref[...]
    h2 = h_ref[...] + (y_ref[slot, 0] * gates[:, 0:1] + y_ref[slot, 1] * gates[:, 1:2])
    h3 = _ple(h2, p_ref[...], wp_ref[...], wg_ref[...])
    o_ref[...] = h3 * _rstd(h3) * g_ref[...]


def _combine(dest3, yb, h, gates, p, wp, wg, g, *, tm):
    t, d = h.shape
    dp = p.shape[2]
    n = t // tm
    row = lambda i: (i, 0)
    const = lambda i: (0, 0)
    dest_spec = lambda index_map: pl.BlockSpec((1, 1, tm * TOP_K), index_map, memory_space=pltpu.SMEM)
    return pl.pallas_call(
        functools.partial(_combine_kernel, tm=tm),
        grid=(n,),
        in_specs=[
            dest_spec(lambda i: (i, 0, 0)),
            dest_spec(lambda i: (jnp.minimum(i + 1, n - 1), 0, 0)),
            pl.BlockSpec(memory_space=pl.ANY),
            pl.BlockSpec((tm, d), row),
            pl.BlockSpec((tm, LANES), row),
            _layer_rows(p, 1, tm),
            _resident((dp, d), const),
            _resident((d, d), const),
            _resident((1, d), const),
        ],
        out_specs=pl.BlockSpec((tm, d), row),
        out_shape=jax.ShapeDtypeStruct((t, d), F32),
        scratch_shapes=[pltpu.VMEM((2, TOP_K, tm, d), F32), pltpu.SemaphoreType.DMA((2,))],
        compiler_params=_params("arbitrary"),
        name="moe_combine",
    )(dest3, dest3, yb, h, gates, p, wp, wg, g)


def _routing_tables(idx, n_tokens, *, tm):
    e_flat = idx.reshape(-1)
    onehot = (e_flat[:, None] == jnp.arange(N_EXPERTS, dtype=jnp.int32)[None, :]).astype(jnp.int32)
    csum = jnp.cumsum(onehot, axis=0)
    counts = csum[-1]
    padded = (counts + EXPERT_TILE - 1) // EXPERT_TILE * EXPERT_TILE
    pend = jnp.cumsum(padded)
    pstart = pend - padded
    dest = jnp.sum(onehot * (csum - 1 + pstart[None, :]), axis=1)
    n_tiles = (n_tokens * TOP_K) // EXPERT_TILE + N_EXPERTS
    tile_row0 = jnp.arange(n_tiles, dtype=jnp.int32) * EXPERT_TILE
    tile_expert = jnp.minimum(jnp.sum((tile_row0[:, None] >= pend[None, :]).astype(jnp.int32), axis=1),
                              N_EXPERTS - 1)
    tile_rows = jnp.clip(counts[tile_expert] - (tile_row0 - pstart[tile_expert]), 0, EXPERT_TILE)
    tile_halves = ((tile_rows + EXPERT_TILE // 2 - 1) // (EXPERT_TILE // 2)).astype(jnp.int32)
    dest3 = dest.astype(jnp.int32).reshape(n_tokens // tm, 1, tm * TOP_K)
    return dest3, tile_expert, tile_halves, pend.astype(jnp.int32), padded.astype(jnp.int32), n_tiles


def kernel(x, p, attn_norm, ffn_norm, w_in_a, b_f, w_o_a, kv_norm, w_kv, w_q_b, w_o_b, w_gu_dense, w_down_dense, router_w, router_b, w_gu_moe, w_down_moe, w_ple_proj, w_ple_gate, final_norm):
    batch, seq, d = x.shape
    t = batch * seq
    n_heads = d // HEAD_DIM
    x2 = x.reshape(t, d)
    p2 = p.reshape(p.shape[0], t, p.shape[-1])
    bf = lambda w: w.astype(BF16)

    ii = lax.broadcasted_iota(jnp.int32, (CUMSUM_CHUNK, CUMSUM_CHUNK), 0)
    jj = lax.broadcasted_iota(jnp.int32, (CUMSUM_CHUNK, CUMSUM_CHUNK), 1)
    tri_le = (ii <= jj).astype(BF16)
    tri_ge = (ii >= jj).astype(BF16)

    w_in = bf(w_in_a[0])
    qkv, lf_t = _qkvf_proj(x2, attn_norm[0:1], w_in, w_in[:, 3 * d:].T, b_f[0].reshape(n_heads, 1))
    c = _forget_cumsum(lf_t, tri_le, seq=seq)
    o0 = _fox_attention(qkv, c, batch=batch, seq=seq, d_model=d)
    gains = jnp.stack([kv_norm, attn_norm[1]])
    h3, kv, q1 = _layer0_tail(o0, x2, p2, bf(w_o_a[0]), ffn_norm[0:1], bf(w_gu_dense[0]), bf(w_down_dense[0]),
                              bf(w_ple_proj[0]), bf(w_ple_gate[0]), gains, bf(w_kv), bf(w_q_b[0]))

    o1 = _sb_attention(q1, kv, tri_ge, batch=batch, seq=seq, d_model=d)
    rw = jnp.pad(router_w[0], ((0, 0), (0, LANES - N_EXPERTS)))
    rw_hi = bf(rw)
    rw_lo = bf(rw - rw_hi.astype(F32))
    rw = jnp.concatenate([rw_hi, rw_hi, rw_lo], axis=0)
    rb = jnp.pad(router_b[0], (0, LANES - N_EXPERTS)).reshape(1, LANES)
    h4, fn1, idx, gates = _oproj_router(o1, bf(w_o_b[0]), h3, ffn_norm[1:2], rw, rb)

    tm = 512
    dest3, tile_expert, tile_halves, pend, padded, n_tiles = _routing_tables(idx[:, :TOP_K], t, tm=tm)
    xb = _dispatch(fn1, dest3, pend, padded, n_tiles * EXPERT_TILE, tm=tm)
    yb = _experts(xb, bf(w_gu_moe[0]), bf(w_down_moe[0]), tile_expert, tile_halves)
    out = _combine(dest3, yb, h4, gates, p2, bf(w_ple_proj[1]), bf(w_ple_gate[1]),
                   final_norm.reshape(1, d), tm=tm)
    return out.reshape(batch, seq, d)
```

```python
import functools
import math

import jax
import jax.numpy as jnp
from jax import lax
from jax.experimental import pallas as pl
from jax.experimental.pallas import tpu as pltpu

HEAD_DIM = 64
LANES = 128
HEADS_PER_BLOCK = LANES // HEAD_DIM
N_EXPERTS = 8
TOP_K = 2
RMS_EPS = 1e-6
LOG2_E = 1.4426950408889634
ATTN_TILE = 256
CUMSUM_CHUNK = 256
EXPERT_TILE = 512
DENSE_CHUNK = 256
EXPERT_CHUNK = 512
VMEM_LIMIT = 56 * 1024 * 1024

BF16 = jnp.bfloat16
F32 = jnp.float32


def _params(*semantics):
    return pltpu.CompilerParams(dimension_semantics=semantics, vmem_limit_bytes=VMEM_LIMIT)


def _dot(a, b):
    return jnp.dot(a, b, preferred_element_type=F32)


def _dot_nt(a, b):
    return lax.dot_general(a, b, (((1,), (1,)), ((), ())), preferred_element_type=F32)


def _rstd(x):
    return lax.rsqrt(jnp.mean(x * x, axis=-1, keepdims=True) + RMS_EPS)


def _log_sigmoid(u):
    return jnp.minimum(u, 0.0) - jnp.log1p(jnp.exp(-jnp.abs(u)))


def _split3(x):
    hi = x.astype(BF16)
    r1 = x - hi.astype(F32)
    mid = r1.astype(BF16)
    lo = (r1 - mid.astype(F32)).astype(BF16)
    return hi, mid, lo


def _resident(block_shape, index_map):
    return pl.BlockSpec(block_shape, index_map, pipeline_mode=pl.Buffered(1))


def _qkvf_kernel(x_ref, g_ref, w_ref, wf_ref, bf_ref, qkv_ref, lf_ref):
    d = x_ref.shape[1]
    x = x_ref[...]
    xn = (x * _rstd(x) * g_ref[...]).astype(BF16)
    f = _dot_nt(wf_ref[...], xn) + bf_ref[...]
    lf_ref[...] = _log_sigmoid(f)
    for part in range(3):
        cols = slice(part * d, (part + 1) * d)
        y = _dot(xn, w_ref[:, cols])
        if part == 0:
            y = y * (1.0 / math.sqrt(HEAD_DIM))
        qkv_ref[:, cols] = y.astype(BF16)


def _qkvf_proj(x, g, w_in, w_f_t, b_f, *, tm=512):
    t, d = x.shape
    n = 3 * d
    h = w_f_t.shape[0]
    const = lambda i: (0, 0)
    return pl.pallas_call(
        _qkvf_kernel,
        grid=(t // tm,),
        in_specs=[
            pl.BlockSpec((tm, d), lambda i: (i, 0)),
            _resident((1, d), const),
            _resident((d, n), const),
            _resident((h, d), const),
            _resident((h, 1), const),
        ],
        out_specs=[
            pl.BlockSpec((tm, n), lambda i: (i, 0)),
            pl.BlockSpec((h, tm), lambda i: (0, i)),
        ],
        out_shape=[
            jax.ShapeDtypeStruct((t, n), BF16),
            jax.ShapeDtypeStruct((h, t), F32),
        ],
        compiler_params=_params("parallel"),
        name="qkvf_proj",
    )(x, g, w_in, w_f_t, b_f)


def _cumsum_kernel(lf_ref, tri_ref, c_ref):
    h, s = lf_ref.shape
    tri = tri_ref[...]
    carry = jnp.zeros((h, 1), F32)
    for c0 in range(0, s, CUMSUM_CHUNK):
        hi, mid, lo = _split3(lf_ref[:, c0:c0 + CUMSUM_CHUNK])
        cs = _dot(lo, tri) + _dot(mid, tri) + _dot(hi, tri) + carry
        for hd in range(h):
            c_ref[hd, :, c0:c0 + CUMSUM_CHUNK] = cs[hd:hd + 1, :]
        carry = cs[:, CUMSUM_CHUNK - 1:CUMSUM_CHUNK]


def _forget_cumsum(lf_t, tri, *, seq):
    h, t = lf_t.shape
    return pl.pallas_call(
        _cumsum_kernel,
        grid=(t // seq,),
        in_specs=[
            pl.BlockSpec((h, seq), lambda b: (0, b)),
            pl.BlockSpec((CUMSUM_CHUNK, CUMSUM_CHUNK), lambda b: (0, 0)),
        ],
        out_specs=pl.BlockSpec((h, 1, seq), lambda b: (0, 0, b)),
        out_shape=jax.ShapeDtypeStruct((h, 1, t), F32),
        compiler_params=_params("parallel"),
        name="forget_cumsum",
    )(lf_t, tri)


def _head_masks(shape):
    lane = lax.broadcasted_iota(jnp.int32, shape, 1)
    return lane < HEAD_DIM


def _split_heads(x2, first_head_lanes, fill):
    return (jnp.where(first_head_lanes, x2, fill), jnp.where(first_head_lanes, fill, x2))


def _stack_heads(q2, first_head_lanes):
    return jnp.concatenate(_split_heads(q2, first_head_lanes, jnp.zeros_like(q2)), axis=0)


def _lane_tile(x, width):
    return jnp.concatenate([x] * (width // LANES), axis=1)


def _fox_kernel(q_ref, k_ref, v_ref, c_ref, o_ref, *scratch):
    tq = ATTN_TILE
    nq = q_ref.shape[0] // tq
    m_refs, acc_refs = scratch[:nq], scratch[nq:]
    first = _head_masks((tq, LANES))

    def tile(qs, m_ref, acc_ref, kt, diagonal):
        k = k_ref[kt * tq:(kt + 1) * tq, :]
        v = v_ref[kt * tq:(kt + 1) * tq, :]
        v_one = jnp.concatenate([v, jnp.ones_like(v)], axis=1)
        s = _dot_nt(qs, k)
        s = jnp.concatenate([s[hd * tq:(hd + 1) * tq] - c_ref[hd, :, kt * tq:(kt + 1) * tq]
                             for hd in range(HEADS_PER_BLOCK)], axis=0)
        if diagonal:
            row = lax.broadcasted_iota(jnp.int32, s.shape, 0) & (tq - 1)
            col = lax.broadcasted_iota(jnp.int32, s.shape, 1)
            s = jnp.where(col <= row, s, -jnp.inf)
        m_prev = m_ref[...]
        m_new = jnp.maximum(m_prev, jnp.max(s, axis=1, keepdims=True))
        alpha = jnp.exp(m_prev - m_new)
        p = jnp.exp(s - _lane_tile(m_new, tq))
        acc_ref[...] = _lane_tile(alpha, 2 * LANES) * acc_ref[...] + _dot(p.astype(BF16), v_one)
        m_ref[...] = m_new

    for qt in range(nq):
        qs = _stack_heads(q_ref[qt * tq:(qt + 1) * tq, :], first)
        m_ref, acc_ref = m_refs[qt], acc_refs[qt]
        m_ref[...] = jnp.full(m_ref.shape, -jnp.inf, F32)
        acc_ref[...] = jnp.zeros(acc_ref.shape, F32)
        for kt in range(qt + 1):
            tile(qs, m_ref, acc_ref, kt, kt == qt)
        out = acc_ref[:, :LANES] / acc_ref[:, LANES:]
        o_ref[qt * tq:(qt + 1) * tq, :] = jnp.where(first, out[:tq], out[tq:]).astype(BF16)


def _fox_attention(qkv, c, *, batch, seq, d_model):
    t = qkv.shape[0]
    n_pairs = d_model // LANES
    tq = ATTN_TILE
    nq = seq // tq
    assert tq & (tq - 1) == 0, "the diagonal mask takes row % tq as row & (tq - 1)"
    rows = HEADS_PER_BLOCK * tq
    return pl.pallas_call(
        _fox_kernel,
        grid=(batch, n_pairs),
        in_specs=[
            pl.BlockSpec((seq, LANES), lambda b, hp: (b, hp)),
            pl.BlockSpec((seq, LANES), lambda b, hp: (b, n_pairs + hp)),
            pl.BlockSpec((seq, LANES), lambda b, hp: (b, 2 * n_pairs + hp)),
            pl.BlockSpec((HEADS_PER_BLOCK, 1, seq), lambda b, hp: (hp, 0, b)),
        ],
        out_specs=pl.BlockSpec((seq, LANES), lambda b, hp: (b, hp)),
        out_shape=jax.ShapeDtypeStruct((t, d_model), BF16),
        scratch_shapes=([pltpu.VMEM((rows, LANES), F32)] * nq + [pltpu.VMEM((rows, 2 * LANES), F32)] * nq),
        compiler_params=_params("parallel", "parallel"),
        name="fox_attention",
    )(qkv, qkv, qkv, c)


def _sb_kernel(q_ref, k_ref, v_ref, tri_ref, o_ref, *scratch):
    tq = ATTN_TILE
    nq = q_ref.shape[0] // tq
    r_refs, acc_refs = scratch[:nq], scratch[nq:]
    sub = CUMSUM_CHUNK
    n_sub = tq // sub
    first = _head_masks((tq, LANES))

    def tile(qs, r_ref, acc_ref, kt, diagonal):
        k = k_ref[kt * tq:(kt + 1) * tq, :]
        v = v_ref[kt * tq:(kt + 1) * tq, :]
        z = _dot_nt(qs, k)
        nlm = jnp.maximum(z, 0.0) + jnp.log(1.0 + jnp.exp2(jnp.abs(z) * (-LOG2_E)))
        if diagonal:
            row = lax.broadcasted_iota(jnp.int32, z.shape, 0) & (tq - 1)
            col = lax.broadcasted_iota(jnp.int32, z.shape, 1)
            visible = col < row
            nlm = jnp.where(visible, nlm, 0.0)
        nlm16 = nlm.astype(BF16)
        p = [_dot(nlm16[:, sb * sub:(sb + 1) * sub], tri_ref[...]) for sb in range(n_sub)]
        r = r_ref[...]
        later = [None] * n_sub
        for sb in reversed(range(n_sub)):
            later[sb] = p[sb] + _lane_tile(r, sub)
            r = r + p[sb][:, 0:1]
        w = jnp.exp(z - jnp.concatenate(later, axis=1))
        if diagonal:
            w = jnp.where(visible, w, 0.0)
        acc_ref[...] += _dot(w.astype(BF16), v)
        r_ref[...] = r

    for qt in range(nq):
        qs = _stack_heads(q_ref[qt * tq:(qt + 1) * tq, :], first)
        r_ref, acc_ref = r_refs[qt], acc_refs[qt]
        r_ref[...] = jnp.zeros(r_ref.shape, F32)
        acc_ref[...] = jnp.zeros(acc_ref.shape, F32)
        for kt in reversed(range(qt + 1)):
            tile(qs, r_ref, acc_ref, kt, kt == qt)
        o_ref[qt * tq:(qt + 1) * tq, :] = jnp.where(first, acc_ref[:tq], acc_ref[tq:]).astype(BF16)


def _sb_attention(q, kv, tri, *, batch, seq, d_model):
    t = q.shape[0]
    n_pairs = d_model // LANES
    tq = ATTN_TILE
    nq = seq // tq
    assert tq & (tq - 1) == 0, "the diagonal mask takes row % tq as row & (tq - 1)"
    stat = pltpu.VMEM((HEADS_PER_BLOCK * tq, LANES), F32)
    return pl.pallas_call(
        _sb_kernel,
        grid=(batch, n_pairs),
        in_specs=[
            pl.BlockSpec((seq, LANES), lambda b, hp: (b, hp)),
            pl.BlockSpec((seq, LANES), lambda b, hp: (b, hp)),
            pl.BlockSpec((seq, LANES), lambda b, hp: (b, n_pairs + hp)),
            pl.BlockSpec(tri.shape, lambda b, hp: (0, 0)),
        ],
        out_specs=pl.BlockSpec((seq, LANES), lambda b, hp: (b, hp)),
        out_shape=jax.ShapeDtypeStruct((t, d_model), BF16),
        scratch_shapes=[stat] * (2 * nq),
        compiler_params=_params("parallel", "parallel"),
        name="sb_attention",
    )(q, kv, kv, tri)


def _oproj_router_kernel(o_ref, w_ref, h_ref, g_ref, rw_ref, rb_ref, h1_ref, fn_ref, idx_ref, gate_ref):
    h1 = h_ref[...] + _dot(o_ref[...], w_ref[...])
    h1_ref[...] = h1
    fn = h1 * _rstd(h1) * g_ref[...]
    fn_ref[...] = fn
    fn_hi = fn.astype(BF16)
    fn_lo = (fn - fn_hi.astype(F32)).astype(BF16)
    lhs = jnp.concatenate([fn_hi, fn_lo, fn_hi], axis=1)
    half = lhs.shape[0] // 2
    logits = jnp.concatenate([_dot(lhs[:half], rw_ref[...]), _dot(lhs[half:], rw_ref[...])], axis=0) + rb_ref[...]
    lane = lax.broadcasted_iota(jnp.int32, logits.shape, 1)
    logits = jnp.where(lane < N_EXPERTS, logits, -jnp.inf)
    lane_f = lane.astype(F32)
    m1 = jnp.max(logits, axis=1, keepdims=True)
    i1 = jnp.min(jnp.where(logits == m1, lane_f, float(LANES)), axis=1, keepdims=True)
    rest = jnp.where(lane_f == i1, -jnp.inf, logits)
    m2 = jnp.max(rest, axis=1, keepdims=True)
    i2 = jnp.min(jnp.where(rest == m2, lane_f, float(LANES)), axis=1, keepdims=True)
    e2 = jnp.exp(m2 - m1)
    g1 = 1.0 / (1.0 + e2)
    g2 = e2 / (1.0 + e2)
    idx_ref[...] = jnp.where(lane == 0, i1, jnp.where(lane == 1, i2, 0.0)).astype(jnp.int32)
    gate_ref[...] = jnp.where(lane == 0, g1, jnp.where(lane == 1, g2, 0.0))


def _oproj_router(o, w_o, h, g, rw, rb, *, tm=512):
    t, d = h.shape
    row = lambda i: (i, 0)
    const = lambda i: (0, 0)
    return pl.pallas_call(
        _oproj_router_kernel,
        grid=(t // tm,),
        in_specs=[pl.BlockSpec((tm, d), row), _resident((d, d), const), pl.BlockSpec((tm, d), row),
                  _resident((1, d), const), _resident(rw.shape, const), _resident((1, LANES), const)],
        out_specs=[pl.BlockSpec((tm, d), row), pl.BlockSpec((tm, d), row),
                   pl.BlockSpec((tm, LANES), row), pl.BlockSpec((tm, LANES), row)],
        out_shape=[jax.ShapeDtypeStruct((t, d), F32), jax.ShapeDtypeStruct((t, d), F32),
                   jax.ShapeDtypeStruct((t, LANES), jnp.int32), jax.ShapeDtypeStruct((t, LANES), F32)],
        compiler_params=_params("parallel"),
        name="oproj_norm_router",
    )(o, w_o, h, g, rw, rb)


def _swiglu_accumulate(x, w_gate_ref, w_up_ref, w_down_ref, out_ref, *, chunk, up_offset=0):
    f = w_down_ref.shape[0]
    for c0 in range(0, f, chunk):
        g = _dot(x, w_gate_ref[:, c0:c0 + chunk])
        u = _dot(x, w_up_ref[:, up_offset + c0:up_offset + c0 + chunk])
        act = (g * (1.0 / (1.0 + jnp.exp(-g))) * u).astype(BF16)
        out_ref[...] += _dot(act, w_down_ref[c0:c0 + chunk, :])


def _ple(h2, p, wp, wg):
    gate = 1.0 / (1.0 + jnp.exp(-_dot(h2.astype(BF16), wg)))
    return h2 + _dot(p.astype(BF16), wp) * gate


def _layer0_tail_kernel(o_ref, x_ref, p_ref, wo_ref, g_ref, wgu_ref, wd_ref, wp_ref, wg_ref, gains_ref,
                        wkv_ref, wq_ref, h3_ref, kv_ref, q_ref):
    h1 = x_ref[...] + _dot(o_ref[...], wo_ref[...])
    fn = (h1 * _rstd(h1) * g_ref[...]).astype(BF16)
    h3_ref[...] = h1
    _swiglu_accumulate(fn, wgu_ref, wgu_ref, wd_ref, h3_ref, chunk=DENSE_CHUNK, up_offset=wd_ref.shape[0])
    h3 = _ple(h3_ref[...], p_ref[...], wp_ref[...], wg_ref[...])
    h3_ref[...] = h3
    y = h3 * _rstd(h3)
    kv_ref[...] = _dot((y * gains_ref[0:1, :]).astype(BF16), wkv_ref[...]).astype(BF16)
    q = _dot((y * gains_ref[1:2, :]).astype(BF16), wq_ref[...])
    q_ref[...] = (q * (1.0 / math.sqrt(HEAD_DIM))).astype(BF16)


def _layer_rows(p, layer, tm):
    return pl.BlockSpec((None, tm, p.shape[2]), lambda i: (layer, i, 0))


def _layer0_tail(o, x, p, w_o, g, w_gu, w_down, wp, wg, gains, w_kv, w_q, *, tm=512):
    t, d = x.shape
    row = lambda i: (i, 0)
    const = lambda i: (0, 0)
    weights = (w_o, g, w_gu, w_down, wp, wg, gains, w_kv, w_q)
    return pl.pallas_call(
        _layer0_tail_kernel,
        grid=(t // tm,),
        in_specs=[pl.BlockSpec((tm, d), row), pl.BlockSpec((tm, d), row), _layer_rows(p, 0, tm)]
                 + [_resident(w.shape, const) for w in weights],
        out_specs=[pl.BlockSpec((tm, d), row), pl.BlockSpec((tm, w_kv.shape[1]), row), pl.BlockSpec((tm, d), row)],
        out_shape=[jax.ShapeDtypeStruct((t, d), F32), jax.ShapeDtypeStruct((t, w_kv.shape[1]), BF16),
                   jax.ShapeDtypeStruct((t, d), BF16)],
        compiler_params=_params("parallel"),
        name="layer0_tail",
    )(o, x, p, *weights)


def _dispatch_kernel(pend_ref, padded_ref, dest_ref, x_ref, xb_hbm, zero_ref, sem, zero_sem, *, tm):
    n_tiles = xb_hbm.shape[0] // EXPERT_TILE

    def zero_tile(tile):
        row0 = pl.multiple_of(tile * EXPERT_TILE, EXPERT_TILE)
        return pltpu.make_async_copy(zero_ref, xb_hbm.at[pl.ds(row0, EXPERT_TILE)], zero_sem)

    def last_tile(e):
        return pend_ref[e] // EXPERT_TILE - 1

    def for_unused_tiles(fn):
        def body(tile, carry):
            fn(tile)
            return carry
        lax.fori_loop(pend_ref[N_EXPERTS - 1] // EXPERT_TILE, n_tiles, body, 0)

    @pl.when(pl.program_id(0) == 0)
    def _():
        zero_ref[...] = jnp.zeros(zero_ref.shape, zero_ref.dtype)
        for e in range(N_EXPERTS):
            pl.when(padded_ref[e] > 0)(lambda e=e: zero_tile(last_tile(e)).start())
        for_unused_tiles(lambda tile: zero_tile(tile).start())
        for e in range(N_EXPERTS):
            pl.when(padded_ref[e] > 0)(lambda e=e: zero_tile(last_tile(e)).wait())
        for_unused_tiles(lambda tile: zero_tile(tile).wait())

    def issue(r, carry):
        for k in range(TOP_K):
            pltpu.make_async_copy(x_ref.at[pl.ds(r, 1)],
                                  xb_hbm.at[pl.ds(dest_ref[0, 0, r * TOP_K + k], 1)], sem).start(priority=k)
        return carry

    lax.fori_loop(0, tm, issue, 0, unroll=8)
    for _ in range(TOP_K):
        pltpu.make_async_copy(x_ref, xb_hbm.at[pl.ds(0, tm)], sem).wait()


def _dispatch(x, dest3, pend, padded, n_rows, *, tm):
    t, d = x.shape
    return pl.pallas_call(
        functools.partial(_dispatch_kernel, tm=tm),
        grid_spec=pltpu.PrefetchScalarGridSpec(
            num_scalar_prefetch=2,
            grid=(t // tm,),
            in_specs=[
                pl.BlockSpec((1, 1, tm * TOP_K), lambda i, pe, pa: (i, 0, 0), memory_space=pltpu.SMEM),
                pl.BlockSpec((tm, d), lambda i, pe, pa: (i, 0)),
            ],
            out_specs=pl.BlockSpec(memory_space=pl.ANY),
            scratch_shapes=[pltpu.VMEM((EXPERT_TILE, d), x.dtype), pltpu.SemaphoreType.DMA,
                            pltpu.SemaphoreType.DMA],
        ),
        out_shape=jax.ShapeDtypeStruct((n_rows, d), x.dtype),
        compiler_params=_params("arbitrary"),
        name="moe_dispatch",
    )(pend, padded, dest3, x)


def _expert_kernel(te_ref, th_ref, x_ref, wgu_hbm, wd_hbm, y_ref, wg_ref, wu_ref, wd_ref,
                   sg_ref, su_ref, sd_ref, sems, *, chunk):
    t = pl.program_id(0)
    e = te_ref[t]
    halves = th_ref[t]
    half = EXPERT_TILE // 2
    fe = wd_ref.shape[0]
    n_chunks = fe // chunk
    y_ref[...] = jnp.zeros(y_ref.shape, F32)

    def chunk_copies(j, slot):
        cols = pl.ds(j * chunk, chunk)
        return (pltpu.make_async_copy(wgu_hbm.at[e, :, cols], sg_ref.at[slot], sems.at[0, slot]),
                pltpu.make_async_copy(wgu_hbm.at[e, :, pl.ds(fe + j * chunk, chunk)], su_ref.at[slot], sems.at[1, slot]),
                pltpu.make_async_copy(wd_hbm.at[e, cols, :], sd_ref.at[slot], sems.at[2, slot]))

    new_expert = jnp.logical_or(t == 0, te_ref[jnp.maximum(t - 1, 0)] != e)

    @pl.when(jnp.logical_and(new_expert, halves > 0))
    def _():
        for c in chunk_copies(0, 0):
            c.start()
        for j in range(n_chunks):
            slot = j % 2
            if j + 1 < n_chunks:
                for c in chunk_copies(j + 1, 1 - slot):
                    c.start()
            for c in chunk_copies(j, slot):
                c.wait()
            wg_ref[:, j * chunk:(j + 1) * chunk] = sg_ref[slot].astype(BF16)
            wu_ref[:, j * chunk:(j + 1) * chunk] = su_ref[slot].astype(BF16)
            wd_ref[j * chunk:(j + 1) * chunk, :] = sd_ref[slot].astype(BF16)

    @pl.when(halves == 2)
    def _():
        _swiglu_accumulate(x_ref[...].astype(BF16), wg_ref, wu_ref, wd_ref, y_ref, chunk=chunk)

    @pl.when(halves == 1)
    def _():
        _swiglu_accumulate(x_ref[:half, :].astype(BF16), wg_ref, wu_ref, wd_ref, y_ref.at[pl.ds(0, half)],
                           chunk=chunk)


def _experts(xb, w_gu, w_down, tile_expert, tile_halves, *, chunk=EXPERT_CHUNK):
    n_rows, d = xb.shape
    fe = w_down.shape[1]
    n_tiles = n_rows // EXPERT_TILE
    return pl.pallas_call(
        functools.partial(_expert_kernel, chunk=chunk),
        grid_spec=pltpu.PrefetchScalarGridSpec(
            num_scalar_prefetch=2,
            grid=(n_tiles,),
            in_specs=[
                pl.BlockSpec((EXPERT_TILE, d), lambda ti, te, th: (ti, 0)),
                pl.BlockSpec(memory_space=pl.ANY),
                pl.BlockSpec(memory_space=pl.ANY),
            ],
            out_specs=pl.BlockSpec((EXPERT_TILE, d), lambda ti, te, th: (ti, 0)),
            scratch_shapes=[
                pltpu.VMEM((d, fe), BF16), pltpu.VMEM((d, fe), BF16), pltpu.VMEM((fe, d), BF16),
                pltpu.VMEM((2, d, chunk), F32), pltpu.VMEM((2, d, chunk), F32), pltpu.VMEM((2, chunk, d), F32),
                pltpu.SemaphoreType.DMA((3, 2)),
            ],
        ),
        out_shape=jax.ShapeDtypeStruct((n_rows, d), F32),
        compiler_params=_params("arbitrary"),
        name="moe_experts",
    )(tile_expert, tile_halves, xb, w_gu, w_down)


def _combine_kernel(dest_ref, dest_next_ref, yb_hbm, h_ref, gate_ref, p_ref, wp_ref, wg_ref, g_ref, o_ref,
                    y_ref, sems, *, tm):
    i = pl.program_id(0)
    slot = i % 2

    def gather(d_ref, s):
        def issue(r, carry):
            for k in range(TOP_K):
                pltpu.make_async_copy(yb_hbm.at[pl.ds(d_ref[0, 0, r * TOP_K + k], 1)],
                                      y_ref.at[s, k, pl.ds(r, 1)], sems.at[s]).start(priority=k)
            return carry
        lax.fori_loop(0, tm, issue, 0, unroll=8)

    pl.when(i == 0)(lambda: gather(dest_ref, 0))
    pl.when(i + 1 < pl.num_programs(0))(lambda: gather(dest_next_ref, 1 - slot))
    for k in range(TOP_K):
        pltpu.make_async_copy(yb_hbm.at[pl.ds(0, tm)], y_ref.at[slot, k], sems.at[slot]).wait()

    gates = gate_ref[...]
    h2 = h_ref[...] + (y_ref[slot, 0] * gates[:, 0:1] + y_ref[slot, 1] * gates[:, 1:2])
    h3 = _ple(h2, p_ref[...], wp_ref[...], wg_ref[...])
    o_ref[...] = h3 * _rstd(h3) * g_ref[...]


def _combine(dest3, yb, h, gates, p, wp, wg, g, *, tm):
    t, d = h.shape
    dp = p.shape[2]
    n = t // tm
    row = lambda i: (i, 0)
    const = lambda i: (0, 0)
    dest_spec = lambda index_map: pl.BlockSpec((1, 1, tm * TOP_K), index_map, memory_space=pltpu.SMEM)
    return pl.pallas_call(
        functools.partial(_combine_kernel, tm=tm),
        grid=(n,),
        in_specs=[
            dest_spec(lambda i: (i, 0, 0)),
            dest_spec(lambda i: (jnp.minimum(i + 1, n - 1), 0, 0)),
            pl.BlockSpec(memory_space=pl.ANY),
            pl.BlockSpec((tm, d), row),
            pl.BlockSpec((tm, LANES), row),
            _layer_rows(p, 1, tm),
            _resident((dp, d), const),
            _resident((d, d), const),
            _resident((1, d), const),
        ],
        out_specs=pl.BlockSpec((tm, d), row),
        out_shape=jax.ShapeDtypeStruct((t, d), F32),
        scratch_shapes=[pltpu.VMEM((2, TOP_K, tm, d), F32), pltpu.SemaphoreType.DMA((2,))],
        compiler_params=_params("arbitrary"),
        name="moe_combine",
    )(dest3, dest3, yb, h, gates, p, wp, wg, g)


def _routing_tables(idx, n_tokens, *, tm):
    e_flat = idx.reshape(-1)
    onehot = (e_flat[:, None] == jnp.arange(N_EXPERTS, dtype=jnp.int32)[None, :]).astype(jnp.int32)
    csum = jnp.cumsum(onehot, axis=0)
    counts = csum[-1]
    padded = (counts + EXPERT_TILE - 1) // EXPERT_TILE * EXPERT_TILE
    pend = jnp.cumsum(padded)
    pstart = pend - padded
    dest = jnp.sum(onehot * (csum - 1 + pstart[None, :]), axis=1)
    n_tiles = (n_tokens * TOP_K) // EXPERT_TILE + N_EXPERTS
    tile_row0 = jnp.arange(n_tiles, dtype=jnp.int32) * EXPERT_TILE
    tile_expert = jnp.minimum(jnp.sum((tile_row0[:, None] >= pend[None, :]).astype(jnp.int32), axis=1),
                              N_EXPERTS - 1)
    tile_rows = jnp.clip(counts[tile_expert] - (tile_row0 - pstart[tile_expert]), 0, EXPERT_TILE)
    tile_halves = ((tile_rows + EXPERT_TILE // 2 - 1) // (EXPERT_TILE // 2)).astype(jnp.int32)
    dest3 = dest.astype(jnp.int32).reshape(n_tokens // tm, 1, tm * TOP_K)
    return dest3, tile_expert, tile_halves, pend.astype(jnp.int32), padded.astype(jnp.int32), n_tiles


def kernel(x, p, attn_norm, ffn_norm, w_in_a, b_f, w_o_a, kv_norm, w_kv, w_q_b, w_o_b, w_gu_dense, w_down_dense, router_w, router_b, w_gu_moe, w_down_moe, w_ple_proj, w_ple_gate, final_norm):
    batch, seq, d = x.shape
    t = batch * seq
    n_heads = d // HEAD_DIM
    x2 = x.reshape(t, d)
    p2 = p.reshape(p.shape[0], t, p.shape[-1])
    bf = lambda w: w.astype(BF16)

    ii = lax.broadcasted_iota(jnp.int32, (CUMSUM_CHUNK, CUMSUM_CHUNK), 0)
    jj = lax.broadcasted_iota(jnp.int32, (CUMSUM_CHUNK, CUMSUM_CHUNK), 1)
    tri_le = (ii <= jj).astype(BF16)
    tri_ge = (ii >= jj).astype(BF16)

    w_in = bf(w_in_a[0])
    qkv, lf_t = _qkvf_proj(x2, attn_norm[0:1], w_in, w_in[:, 3 * d:].T, b_f[0].reshape(n_heads, 1))
    c = _forget_cumsum(lf_t, tri_le, seq=seq)
    o0 = _fox_attention(qkv, c, batch=batch, seq=seq, d_model=d)
    gains = jnp.stack([kv_norm, attn_norm[1]])
    h3, kv, q1 = _layer0_tail(o0, x2, p2, bf(w_o_a[0]), ffn_norm[0:1], bf(w_gu_dense[0]), bf(w_down_dense[0]),
                              bf(w_ple_proj[0]), bf(w_ple_gate[0]), gains, bf(w_kv), bf(w_q_b[0]))

    o1 = _sb_attention(q1, kv, tri_ge, batch=batch, seq=seq, d_model=d)
    rw = jnp.pad(router_w[0], ((0, 0), (0, LANES - N_EXPERTS)))
    rw_hi = bf(rw)
    rw_lo = bf(rw - rw_hi.astype(F32))
    rw = jnp.concatenate([rw_hi, rw_hi, rw_lo], axis=0)
    rb = jnp.pad(router_b[0], (0, LANES - N_EXPERTS)).reshape(1, LANES)
    h4, fn1, idx, gates = _oproj_router(o1, bf(w_o_b[0]), h3, ffn_norm[1:2], rw, rb)

    tm = 512
    dest3, tile_expert, tile_halves, pend, padded, n_tiles = _routing_tables(idx[:, :TOP_K], t, tm=tm)
    xb = _dispatch(fn1, dest3, pend, padded, n_tiles * EXPERT_TILE, tm=tm)
    yb = _experts(xb, w_gu_moe[0], w_down_moe[0], tile_expert, tile_halves)
    out = _combine(dest3, yb, h4, gates, p2, bf(w_ple_proj[1]), bf(w_ple_gate[1]),
                   final_norm.reshape(1, d), tm=tm)
    return out.reshape(batch, seq, d)
```

```python
import functools
import math

import jax
import jax.numpy as jnp
from jax import lax
from jax.experimental import pallas as pl
from jax.experimental.pallas import tpu as pltpu

HEAD_DIM = 64
LANES = 128
HEADS_PER_BLOCK = LANES // HEAD_DIM
N_EXPERTS = 8
TOP_K = 2
RMS_EPS = 1e-6
LOG2_E = 1.4426950408889634
ATTN_TILE = 256
CUMSUM_CHUNK = 256
EXPERT_TILE = 512
DENSE_CHUNK = 256
EXPERT_CHUNK = 512
VMEM_LIMIT = 56 * 1024 * 1024

BF16 = jnp.bfloat16
F32 = jnp.float32


def _params(*semantics):
    return pltpu.CompilerParams(dimension_semantics=semantics, vmem_limit_bytes=VMEM_LIMIT)


def _dot(a, b):
    return jnp.dot(a, b, preferred_element_type=F32)


def _dot_nt(a, b):
    return lax.dot_general(a, b, (((1,), (1,)), ((), ())), preferred_element_type=F32)


def _rstd(x):
    return lax.rsqrt(jnp.mean(x * x, axis=-1, keepdims=True) + RMS_EPS)


def _log_sigmoid(u):
    return jnp.minimum(u, 0.0) - jnp.log1p(jnp.exp(-jnp.abs(u)))


def _split3(x):
    hi = x.astype(BF16)
    r1 = x - hi.astype(F32)
    mid = r1.astype(BF16)
    lo = (r1 - mid.astype(F32)).astype(BF16)
    return hi, mid, lo


def _resident(block_shape, index_map):
    return pl.BlockSpec(block_shape, index_map, pipeline_mode=pl.Buffered(1))


def _qkvf_kernel(x_ref, g_ref, w_ref, wf_ref, bf_ref, qkv_ref, lf_ref):
    d = x_ref.shape[1]
    x = x_ref[...]
    xn = (x * _rstd(x) * g_ref[...]).astype(BF16)
    f = _dot_nt(wf_ref[...], xn) + bf_ref[...]
    lf_ref[...] = _log_sigmoid(f)
    for part in range(3):
        cols = slice(part * d, (part + 1) * d)
        y = _dot(xn, w_ref[:, cols])
        if part == 0:
            y = y * (1.0 / math.sqrt(HEAD_DIM))
        qkv_ref[:, cols] = y.astype(BF16)


def _qkvf_proj(x, g, w_in, w_f_t, b_f, *, tm=1024):
    t, d = x.shape
    n = 3 * d
    h = w_f_t.shape[0]
    const = lambda i: (0, 0)
    return pl.pallas_call(
        _qkvf_kernel,
        grid=(t // tm,),
        in_specs=[
            pl.BlockSpec((tm, d), lambda i: (i, 0)),
            _resident((1, d), const),
            _resident((d, n), const),
            _resident((h, d), const),
            _resident((h, 1), const),
        ],
        out_specs=[
            pl.BlockSpec((tm, n), lambda i: (i, 0)),
            pl.BlockSpec((h, tm), lambda i: (0, i)),
        ],
        out_shape=[
            jax.ShapeDtypeStruct((t, n), BF16),
            jax.ShapeDtypeStruct((h, t), F32),
        ],
        compiler_params=_params("parallel"),
        name="qkvf_proj",
    )(x, g, w_in, w_f_t, b_f)


def _cumsum_kernel(lf_ref, tri_ref, c_ref):
    h, s = lf_ref.shape
    tri = tri_ref[...]
    carry = jnp.zeros((h, 1), F32)
    for c0 in range(0, s, CUMSUM_CHUNK):
        hi, mid, lo = _split3(lf_ref[:, c0:c0 + CUMSUM_CHUNK])
        cs = _dot(lo, tri) + _dot(mid, tri) + _dot(hi, tri) + carry
        for hd in range(h):
            c_ref[hd, :, c0:c0 + CUMSUM_CHUNK] = cs[hd:hd + 1, :]
        carry = cs[:, CUMSUM_CHUNK - 1:CUMSUM_CHUNK]


def _forget_cumsum(lf_t, tri, *, seq):
    h, t = lf_t.shape
    return pl.pallas_call(
        _cumsum_kernel,
        grid=(t // seq,),
        in_specs=[
            pl.BlockSpec((h, seq), lambda b: (0, b)),
            pl.BlockSpec((CUMSUM_CHUNK, CUMSUM_CHUNK), lambda b: (0, 0)),
        ],
        out_specs=pl.BlockSpec((h, 1, seq), lambda b: (0, 0, b)),
        out_shape=jax.ShapeDtypeStruct((h, 1, t), F32),
        compiler_params=_params("parallel"),
        name="forget_cumsum",
    )(lf_t, tri)


def _head_masks(shape):
    lane = lax.broadcasted_iota(jnp.int32, shape, 1)
    return lane < HEAD_DIM


def _split_heads(x2, first_head_lanes, fill):
    return (jnp.where(first_head_lanes, x2, fill), jnp.where(first_head_lanes, fill, x2))


def _stack_heads(q2, first_head_lanes):
    return jnp.concatenate(_split_heads(q2, first_head_lanes, jnp.zeros_like(q2)), axis=0)


def _lane_tile(x, width):
    return jnp.concatenate([x] * (width // LANES), axis=1)


def _fox_kernel(q_ref, k_ref, v_ref, c_ref, o_ref, *scratch):
    tq = ATTN_TILE
    nq = q_ref.shape[0] // tq
    m_refs, acc_refs = scratch[:nq], scratch[nq:]
    first = _head_masks((tq, LANES))

    def tile(qs, m_ref, acc_ref, kt, diagonal):
        k = k_ref[kt * tq:(kt + 1) * tq, :]
        v = v_ref[kt * tq:(kt + 1) * tq, :]
        v_one = jnp.concatenate([v, jnp.ones_like(v)], axis=1)
        s = _dot_nt(qs, k)
        s = jnp.concatenate([s[hd * tq:(hd + 1) * tq] - c_ref[hd, :, kt * tq:(kt + 1) * tq]
                             for hd in range(HEADS_PER_BLOCK)], axis=0)
        if diagonal:
            row = lax.broadcasted_iota(jnp.int32, s.shape, 0) & (tq - 1)
            col = lax.broadcasted_iota(jnp.int32, s.shape, 1)
            s = jnp.where(col <= row, s, -jnp.inf)
        m_prev = m_ref[...]
        m_new = jnp.maximum(m_prev, jnp.max(s, axis=1, keepdims=True))
        alpha = jnp.exp(m_prev - m_new)
        p = jnp.exp(s - _lane_tile(m_new, tq))
        acc_ref[...] = _lane_tile(alpha, 2 * LANES) * acc_ref[...] + _dot(p.astype(BF16), v_one)
        m_ref[...] = m_new

    for qt in range(nq):
        qs = _stack_heads(q_ref[qt * tq:(qt + 1) * tq, :], first)
        m_ref, acc_ref = m_refs[qt], acc_refs[qt]
        m_ref[...] = jnp.full(m_ref.shape, -jnp.inf, F32)
        acc_ref[...] = jnp.zeros(acc_ref.shape, F32)
        for kt in range(qt + 1):
            tile(qs, m_ref, acc_ref, kt, kt == qt)
        out = acc_ref[:, :LANES] / acc_ref[:, LANES:]
        o_ref[qt * tq:(qt + 1) * tq, :] = jnp.where(first, out[:tq], out[tq:]).astype(BF16)


def _fox_attention(qkv, c, *, batch, seq, d_model):
    t = qkv.shape[0]
    n_pairs = d_model // LANES
    tq = ATTN_TILE
    nq = seq // tq
    assert tq & (tq - 1) == 0, "the diagonal mask takes row % tq as row & (tq - 1)"
    rows = HEADS_PER_BLOCK * tq
    return pl.pallas_call(
        _fox_kernel,
        grid=(batch, n_pairs),
        in_specs=[
            pl.BlockSpec((seq, LANES), lambda b, hp: (b, hp)),
            pl.BlockSpec((seq, LANES), lambda b, hp: (b, n_pairs + hp)),
            pl.BlockSpec((seq, LANES), lambda b, hp: (b, 2 * n_pairs + hp)),
            pl.BlockSpec((HEADS_PER_BLOCK, 1, seq), lambda b, hp: (hp, 0, b)),
        ],
        out_specs=pl.BlockSpec((seq, LANES), lambda b, hp: (b, hp)),
        out_shape=jax.ShapeDtypeStruct((t, d_model), BF16),
        scratch_shapes=([pltpu.VMEM((rows, LANES), F32)] * nq + [pltpu.VMEM((rows, 2 * LANES), F32)] * nq),
        compiler_params=_params("parallel", "parallel"),
        name="fox_attention",
    )(qkv, qkv, qkv, c)


def _sb_kernel(q_ref, k_ref, v_ref, tri_ref, o_ref, *scratch):
    tq = ATTN_TILE
    nq = q_ref.shape[0] // tq
    r_refs, acc_refs = scratch[:nq], scratch[nq:]
    sub = CUMSUM_CHUNK
    n_sub = tq // sub
    first = _head_masks((tq, LANES))

    def tile(qs, r_ref, acc_ref, kt, diagonal):
        k = k_ref[kt * tq:(kt + 1) * tq, :]
        v = v_ref[kt * tq:(kt + 1) * tq, :]
        z = _dot_nt(qs, k)
        nlm = jnp.maximum(z, 0.0) + jnp.log(1.0 + jnp.exp2(jnp.abs(z) * (-LOG2_E)))
        if diagonal:
            row = lax.broadcasted_iota(jnp.int32, z.shape, 0) & (tq - 1)
            col = lax.broadcasted_iota(jnp.int32, z.shape, 1)
            visible = col < row
            nlm = jnp.where(visible, nlm, 0.0)
        nlm16 = nlm.astype(BF16)
        p = [_dot(nlm16[:, sb * sub:(sb + 1) * sub], tri_ref[...]) for sb in range(n_sub)]
        r = r_ref[...]
        later = [None] * n_sub
        for sb in reversed(range(n_sub)):
            later[sb] = p[sb] + _lane_tile(r, sub)
            r = r + p[sb][:, 0:1]
        w = jnp.exp(z - jnp.concatenate(later, axis=1))
        if diagonal:
            w = jnp.where(visible, w, 0.0)
        acc_ref[...] += _dot(w.astype(BF16), v)
        r_ref[...] = r

    for qt in range(nq):
        qs = _stack_heads(q_ref[qt * tq:(qt + 1) * tq, :], first)
        r_ref, acc_ref = r_refs[qt], acc_refs[qt]
        r_ref[...] = jnp.zeros(r_ref.shape, F32)
        acc_ref[...] = jnp.zeros(acc_ref.shape, F32)
        for kt in reversed(range(qt + 1)):
            tile(qs, r_ref, acc_ref, kt, kt == qt)
        o_ref[qt * tq:(qt + 1) * tq, :] = jnp.where(first, acc_ref[:tq], acc_ref[tq:]).astype(BF16)


def _sb_attention(q, kv, tri, *, batch, seq, d_model):
    t = q.shape[0]
    n_pairs = d_model // LANES
    tq = ATTN_TILE
    nq = seq // tq
    assert tq & (tq - 1) == 0, "the diagonal mask takes row % tq as row & (tq - 1)"
    stat = pltpu.VMEM((HEADS_PER_BLOCK * tq, LANES), F32)
    return pl.pallas_call(
        _sb_kernel,
        grid=(batch, n_pairs),
        in_specs=[
            pl.BlockSpec((seq, LANES), lambda b, hp: (b, hp)),
            pl.BlockSpec((seq, LANES), lambda b, hp: (b, hp)),
            pl.BlockSpec((seq, LANES), lambda b, hp: (b, n_pairs + hp)),
            pl.BlockSpec(tri.shape, lambda b, hp: (0, 0)),
        ],
        out_specs=pl.BlockSpec((seq, LANES), lambda b, hp: (b, hp)),
        out_shape=jax.ShapeDtypeStruct((t, d_model), BF16),
        scratch_shapes=[stat] * (2 * nq),
        compiler_params=_params("parallel", "parallel"),
        name="sb_attention",
    )(q, kv, kv, tri)


def _oproj_router_kernel(o_ref, w_ref, h_ref, g_ref, rw_ref, rb_ref, h1_ref, fn_ref, idx_ref, gate_ref):
    h1 = h_ref[...] + _dot(o_ref[...], w_ref[...])
    h1_ref[...] = h1
    fn = h1 * _rstd(h1) * g_ref[...]
    fn_ref[...] = fn
    fn_hi = fn.astype(BF16)
    fn_lo = (fn - fn_hi.astype(F32)).astype(BF16)
    lhs = jnp.concatenate([fn_hi, fn_lo, fn_hi], axis=1)
    half = lhs.shape[0] // 2
    logits = jnp.concatenate([_dot(lhs[:half], rw_ref[...]), _dot(lhs[half:], rw_ref[...])], axis=0) + rb_ref[...]
    lane = lax.broadcasted_iota(jnp.int32, logits.shape, 1)
    logits = jnp.where(lane < N_EXPERTS, logits, -jnp.inf)
    lane_f = lane.astype(F32)
    m1 = jnp.max(logits, axis=1, keepdims=True)
    i1 = jnp.min(jnp.where(logits == m1, lane_f, float(LANES)), axis=1, keepdims=True)
    rest = jnp.where(lane_f == i1, -jnp.inf, logits)
    m2 = jnp.max(rest, axis=1, keepdims=True)
    i2 = jnp.min(jnp.where(rest == m2, lane_f, float(LANES)), axis=1, keepdims=True)
    e2 = jnp.exp(m2 - m1)
    g1 = 1.0 / (1.0 + e2)
    g2 = e2 / (1.0 + e2)
    idx_ref[...] = jnp.where(lane == 0, i1, jnp.where(lane == 1, i2, 0.0)).astype(jnp.int32)
    gate_ref[...] = jnp.where(lane == 0, g1, jnp.where(lane == 1, g2, 0.0))


def _oproj_router(o, w_o, h, g, rw, rb, *, tm=1024):
    t, d = h.shape
    row = lambda i: (i, 0)
    const = lambda i: (0, 0)
    return pl.pallas_call(
        _oproj_router_kernel,
        grid=(t // tm,),
        in_specs=[pl.BlockSpec((tm, d), row), _resident((d, d), const), pl.BlockSpec((tm, d), row),
                  _resident((1, d), const), _resident(rw.shape, const), _resident((1, LANES), const)],
        out_specs=[pl.BlockSpec((tm, d), row), pl.BlockSpec((tm, d), row),
                   pl.BlockSpec((tm, LANES), row), pl.BlockSpec((tm, LANES), row)],
        out_shape=[jax.ShapeDtypeStruct((t, d), F32), jax.ShapeDtypeStruct((t, d), F32),
                   jax.ShapeDtypeStruct((t, LANES), jnp.int32), jax.ShapeDtypeStruct((t, LANES), F32)],
        compiler_params=_params("parallel"),
        name="oproj_norm_router",
    )(o, w_o, h, g, rw, rb)


def _swiglu_accumulate(x, w_gate_ref, w_up_ref, w_down_ref, out_ref, *, chunk, up_offset=0, before_chunk=None):
    f = w_down_ref.shape[0]
    for c0 in range(0, f, chunk):
        if before_chunk is not None:
            before_chunk(c0 // chunk)
        g = _dot(x, w_gate_ref[:, c0:c0 + chunk])
        u = _dot(x, w_up_ref[:, up_offset + c0:up_offset + c0 + chunk])
        act = (g * (1.0 / (1.0 + jnp.exp(-g))) * u).astype(BF16)
        out_ref[...] += _dot(act, w_down_ref[c0:c0 + chunk, :])


def _ple(h2, p, wp, wg):
    gate = 1.0 / (1.0 + jnp.exp(-_dot(h2.astype(BF16), wg)))
    return h2 + _dot(p.astype(BF16), wp) * gate


def _layer0_tail_kernel(o_ref, x_ref, p_ref, wo_ref, g_ref, wgu_ref, wd_ref, wp_ref, wg_ref, gains_ref,
                        wkv_ref, wq_ref, h3_ref, kv_ref, q_ref):
    h1 = x_ref[...] + _dot(o_ref[...], wo_ref[...])
    fn = (h1 * _rstd(h1) * g_ref[...]).astype(BF16)
    h3_ref[...] = h1
    _swiglu_accumulate(fn, wgu_ref, wgu_ref, wd_ref, h3_ref, chunk=DENSE_CHUNK, up_offset=wd_ref.shape[0])
    h3 = _ple(h3_ref[...], p_ref[...], wp_ref[...], wg_ref[...])
    h3_ref[...] = h3
    y = h3 * _rstd(h3)
    kv_ref[...] = _dot((y * gains_ref[0:1, :]).astype(BF16), wkv_ref[...]).astype(BF16)
    q = _dot((y * gains_ref[1:2, :]).astype(BF16), wq_ref[...])
    q_ref[...] = (q * (1.0 / math.sqrt(HEAD_DIM))).astype(BF16)


def _layer_rows(p, layer, tm):
    return pl.BlockSpec((None, tm, p.shape[2]), lambda i: (layer, i, 0))


def _layer0_tail(o, x, p, w_o, g, w_gu, w_down, wp, wg, gains, w_kv, w_q, *, tm=512):
    t, d = x.shape
    row = lambda i: (i, 0)
    const = lambda i: (0, 0)
    weights = (w_o, g, w_gu, w_down, wp, wg, gains, w_kv, w_q)
    return pl.pallas_call(
        _layer0_tail_kernel,
        grid=(t // tm,),
        in_specs=[pl.BlockSpec((tm, d), row), pl.BlockSpec((tm, d), row), _layer_rows(p, 0, tm)]
                 + [_resident(w.shape, const) for w in weights],
        out_specs=[pl.BlockSpec((tm, d), row), pl.BlockSpec((tm, w_kv.shape[1]), row), pl.BlockSpec((tm, d), row)],
        out_shape=[jax.ShapeDtypeStruct((t, d), F32), jax.ShapeDtypeStruct((t, w_kv.shape[1]), BF16),
                   jax.ShapeDtypeStruct((t, d), BF16)],
        compiler_params=_params("parallel"),
        name="layer0_tail",
    )(o, x, p, *weights)


def _dispatch_kernel(pend_ref, padded_ref, dest_ref, x_ref, xb_hbm, zero_ref, sem, zero_sem, *, tm):
    n_tiles = xb_hbm.shape[0] // EXPERT_TILE

    def zero_tile(tile):
        row0 = pl.multiple_of(tile * EXPERT_TILE, EXPERT_TILE)
        return pltpu.make_async_copy(zero_ref, xb_hbm.at[pl.ds(row0, EXPERT_TILE)], zero_sem)

    def last_tile(e):
        return pend_ref[e] // EXPERT_TILE - 1

    def for_unused_tiles(fn):
        def body(tile, carry):
            fn(tile)
            return carry
        lax.fori_loop(pend_ref[N_EXPERTS - 1] // EXPERT_TILE, n_tiles, body, 0)

    @pl.when(pl.program_id(0) == 0)
    def _():
        zero_ref[...] = jnp.zeros(zero_ref.shape, zero_ref.dtype)
        for e in range(N_EXPERTS):
            pl.when(padded_ref[e] > 0)(lambda e=e: zero_tile(last_tile(e)).start())
        for_unused_tiles(lambda tile: zero_tile(tile).start())
        for e in range(N_EXPERTS):
            pl.when(padded_ref[e] > 0)(lambda e=e: zero_tile(last_tile(e)).wait())
        for_unused_tiles(lambda tile: zero_tile(tile).wait())

    def issue(r, carry):
        for k in range(TOP_K):
            pltpu.make_async_copy(x_ref.at[pl.ds(r, 1)],
                                  xb_hbm.at[pl.ds(dest_ref[0, 0, r * TOP_K + k], 1)], sem).start(priority=k)
        return carry

    lax.fori_loop(0, tm, issue, 0, unroll=8)
    for _ in range(TOP_K):
        pltpu.make_async_copy(x_ref, xb_hbm.at[pl.ds(0, tm)], sem).wait()


def _dispatch(x, dest3, pend, padded, n_rows, *, tm):
    t, d = x.shape
    return pl.pallas_call(
        functools.partial(_dispatch_kernel, tm=tm),
        grid_spec=pltpu.PrefetchScalarGridSpec(
            num_scalar_prefetch=2,
            grid=(t // tm,),
            in_specs=[
                pl.BlockSpec((1, 1, tm * TOP_K), lambda i, pe, pa: (i, 0, 0), memory_space=pltpu.SMEM),
                pl.BlockSpec((tm, d), lambda i, pe, pa: (i, 0)),
            ],
            out_specs=pl.BlockSpec(memory_space=pl.ANY),
            scratch_shapes=[pltpu.VMEM((EXPERT_TILE, d), x.dtype), pltpu.SemaphoreType.DMA,
                            pltpu.SemaphoreType.DMA],
        ),
        out_shape=jax.ShapeDtypeStruct((n_rows, d), x.dtype),
        compiler_params=_params("arbitrary"),
        name="moe_dispatch",
    )(pend, padded, dest3, x)


def _expert_kernel(te_ref, th_ref, x_ref, wgu_hbm, wd_hbm, y_ref, wg_ref, wu_ref, wd_ref,
                   sg_ref, su_ref, sd_ref, sems, *, chunk):
    t = pl.program_id(0)
    e = te_ref[t]
    halves = th_ref[t]
    half = EXPERT_TILE // 2
    fe = wd_ref.shape[0]
    n_chunks = fe // chunk
    y_ref[...] = jnp.zeros(y_ref.shape, F32)

    def chunk_copies(j, slot):
        cols = pl.ds(j * chunk, chunk)
        return (pltpu.make_async_copy(wgu_hbm.at[e, :, cols], sg_ref.at[slot], sems.at[0, slot]),
                pltpu.make_async_copy(wgu_hbm.at[e, :, pl.ds(fe + j * chunk, chunk)], su_ref.at[slot], sems.at[1, slot]),
                pltpu.make_async_copy(wd_hbm.at[e, cols, :], sd_ref.at[slot], sems.at[2, slot]))

    def load_chunk(j):
        slot = j % 2
        if j + 1 < n_chunks:
            for c in chunk_copies(j + 1, 1 - slot):
                c.start()
        for c in chunk_copies(j, slot):
            c.wait()
        wg_ref[:, j * chunk:(j + 1) * chunk] = sg_ref[slot].astype(BF16)
        wu_ref[:, j * chunk:(j + 1) * chunk] = su_ref[slot].astype(BF16)
        wd_ref[j * chunk:(j + 1) * chunk, :] = sd_ref[slot].astype(BF16)

    def run(x, out_ref, streaming):
        if streaming:
            for c in chunk_copies(0, 0):
                c.start()
        _swiglu_accumulate(x.astype(BF16), wg_ref, wu_ref, wd_ref, out_ref, chunk=chunk,
                           before_chunk=load_chunk if streaming else None)

    new_expert = jnp.logical_or(t == 0, te_ref[jnp.maximum(t - 1, 0)] != e)
    for streaming in (True, False):
        first = new_expert if streaming else jnp.logical_not(new_expert)
        pl.when(jnp.logical_and(first, halves == 2))(lambda s=streaming: run(x_ref[...], y_ref, s))
        pl.when(jnp.logical_and(first, halves == 1))(
            lambda s=streaming: run(x_ref[:half, :], y_ref.at[pl.ds(0, half)], s))


def _experts(xb, w_gu, w_down, tile_expert, tile_halves, *, chunk=EXPERT_CHUNK):
    n_rows, d = xb.shape
    fe = w_down.shape[1]
    n_tiles = n_rows // EXPERT_TILE
    return pl.pallas_call(
        functools.partial(_expert_kernel, chunk=chunk),
        grid_spec=pltpu.PrefetchScalarGridSpec(
            num_scalar_prefetch=2,
            grid=(n_tiles,),
            in_specs=[
                pl.BlockSpec((EXPERT_TILE, d), lambda ti, te, th: (ti, 0)),
                pl.BlockSpec(memory_space=pl.ANY),
                pl.BlockSpec(memory_space=pl.ANY),
            ],
            out_specs=pl.BlockSpec((EXPERT_TILE, d), lambda ti, te, th: (ti, 0)),
            scratch_shapes=[
                pltpu.VMEM((d, fe), BF16), pltpu.VMEM((d, fe), BF16), pltpu.VMEM((fe, d), BF16),
                pltpu.VMEM((2, d, chunk), F32), pltpu.VMEM((2, d, chunk), F32), pltpu.VMEM((2, chunk, d), F32),
                pltpu.SemaphoreType.DMA((3, 2)),
            ],
        ),
        out_shape=jax.ShapeDtypeStruct((n_rows, d), F32),
        compiler_params=_params("arbitrary"),
        name="moe_experts",
    )(tile_expert, tile_halves, xb, w_gu, w_down)


def _combine_kernel(dest_ref, dest_next_ref, yb_hbm, h_ref, gate_ref, p_ref, wp_ref, wg_ref, g_ref, o_ref,
                    y_ref, sems, *, tm):
    i = pl.program_id(0)
    slot = i % 2

    def gather(d_ref, s):
        def issue(r, carry):
            for k in range(TOP_K):
                pltpu.make_async_copy(yb_hbm.at[pl.ds(d_ref[0, 0, r * TOP_K + k], 1)],
                                      y_ref.at[s, k, pl.ds(r, 1)], sems.at[s]).start(priority=k)
            return carry
        lax.fori_loop(0, tm, issue, 0, unroll=8)

    pl.when(i == 0)(lambda: gather(dest_ref, 0))
    pl.when(i + 1 < pl.num_programs(0))(lambda: gather(dest_next_ref, 1 - slot))
    for k in range(TOP_K):
        pltpu.make_async_copy(yb_hbm.at[pl.ds(0, tm)], y_ref.at[slot, k], sems.at[slot]).wait()

    gates = gate_ref[...]
    h2 = h_ref[...] + (y_ref[slot, 0] * gates[:, 0:1] + y_ref[slot, 1] * gates[:, 1:2])
    h3 = _ple(h2, p_ref[...], wp_ref[...], wg_ref[...])
    o_ref[...] = h3 * _rstd(h3) * g_ref[...]


def _combine(dest3, yb, h, gates, p, wp, wg, g, *, tm):
    t, d = h.shape
    dp = p.shape[2]
    n = t // tm
    row = lambda i: (i, 0)
    const = lambda i: (0, 0)
    dest_spec = lambda index_map: pl.BlockSpec((1, 1, tm * TOP_K), index_map, memory_space=pltpu.SMEM)
    return pl.pallas_call(
        functools.partial(_combine_kernel, tm=tm),
        grid=(n,),
        in_specs=[
            dest_spec(lambda i: (i, 0, 0)),
            dest_spec(lambda i: (jnp.minimum(i + 1, n - 1), 0, 0)),
            pl.BlockSpec(memory_space=pl.ANY),
            pl.BlockSpec((tm, d), row),
            pl.BlockSpec((tm, LANES), row),
            _layer_rows(p, 1, tm),
            _resident((dp, d), const),
            _resident((d, d), const),
            _resident((1, d), const),
        ],
        out_specs=pl.BlockSpec((tm, d), row),
        out_shape=jax.ShapeDtypeStruct((t, d), F32),
        scratch_shapes=[pltpu.VMEM((2, TOP_K, tm, d), F32), pltpu.SemaphoreType.DMA((2,))],
        compiler_params=_params("arbitrary"),
        name="moe_combine",
    )(dest3, dest3, yb, h, gates, p, wp, wg, g)


def _routing_tables(idx, n_tokens, *, tm):
    e_flat = idx.reshape(-1)
    onehot = (e_flat[:, None] == jnp.arange(N_EXPERTS, dtype=jnp.int32)[None, :]).astype(jnp.int32)
    csum = jnp.cumsum(onehot, axis=0)
    counts = csum[-1]
    padded = (counts + EXPERT_TILE - 1) // EXPERT_TILE * EXPERT_TILE
    pend = jnp.cumsum(padded)
    pstart = pend - padded
    dest = jnp.sum(onehot * (csum - 1 + pstart[None, :]), axis=1)
    n_tiles = (n_tokens * TOP_K) // EXPERT_TILE + N_EXPERTS
    tile_row0 = jnp.arange(n_tiles, dtype=jnp.int32) * EXPERT_TILE
    tile_expert = jnp.minimum(jnp.sum((tile_row0[:, None] >= pend[None, :]).astype(jnp.int32), axis=1),
                              N_EXPERTS - 1)
    tile_rows = jnp.clip(counts[tile_expert] - (tile_row0 - pstart[tile_expert]), 0, EXPERT_TILE)
    tile_halves = ((tile_rows + EXPERT_TILE // 2 - 1) // (EXPERT_TILE // 2)).astype(jnp.int32)
    dest3 = dest.astype(jnp.int32).reshape(n_tokens // tm, 1, tm * TOP_K)
    return dest3, tile_expert, tile_halves, pend.astype(jnp.int32), padded.astype(jnp.int32), n_tiles


def kernel(x, p, attn_norm, ffn_norm, w_in_a, b_f, w_o_a, kv_norm, w_kv, w_q_b, w_o_b, w_gu_dense, w_down_dense, router_w, router_b, w_gu_moe, w_down_moe, w_ple_proj, w_ple_gate, final_norm):
    batch, seq, d = x.shape
    t = batch * seq
    n_heads = d // HEAD_DIM
    x2 = x.reshape(t, d)
    p2 = p.reshape(p.shape[0], t, p.shape[-1])
    bf = lambda w: w.astype(BF16)

    ii = lax.broadcasted_iota(jnp.int32, (CUMSUM_CHUNK, CUMSUM_CHUNK), 0)
    jj = lax.broadcasted_iota(jnp.int32, (CUMSUM_CHUNK, CUMSUM_CHUNK), 1)
    tri_le = (ii <= jj).astype(BF16)
    tri_ge = (ii >= jj).astype(BF16)

    w_in = bf(w_in_a[0])
    qkv, lf_t = _qkvf_proj(x2, attn_norm[0:1], w_in, w_in[:, 3 * d:].T, b_f[0].reshape(n_heads, 1))
    c = _forget_cumsum(lf_t, tri_le, seq=seq)
    o0 = _fox_attention(qkv, c, batch=batch, seq=seq, d_model=d)
    gains = jnp.stack([kv_norm, attn_norm[1]])
    h3, kv, q1 = _layer0_tail(o0, x2, p2, bf(w_o_a[0]), ffn_norm[0:1], bf(w_gu_dense[0]), bf(w_down_dense[0]),
                              bf(w_ple_proj[0]), bf(w_ple_gate[0]), gains, bf(w_kv), bf(w_q_b[0]))

    o1 = _sb_attention(q1, kv, tri_ge, batch=batch, seq=seq, d_model=d)
    rw = jnp.pad(router_w[0], ((0, 0), (0, LANES - N_EXPERTS)))
    rw_hi = bf(rw)
    rw_lo = bf(rw - rw_hi.astype(F32))
    rw = jnp.concatenate([rw_hi, rw_hi, rw_lo], axis=0)
    rb = jnp.pad(router_b[0], (0, LANES - N_EXPERTS)).reshape(1, LANES)
    h4, fn1, idx, gates = _oproj_router(o1, bf(w_o_b[0]), h3, ffn_norm[1:2], rw, rb)

    tm = 512
    dest3, tile_expert, tile_halves, pend, padded, n_tiles = _routing_tables(idx[:, :TOP_K], t, tm=tm)
    xb = _dispatch(fn1, dest3, pend, padded, n_tiles * EXPERT_TILE, tm=tm)
    yb = _experts(xb, w_gu_moe[0], w_down_moe[0], tile_expert, tile_halves)
    out = _combine(dest3, yb, h4, gates, p2, bf(w_ple_proj[1]), bf(w_ple_gate[1]),
                   final_norm.reshape(1, d), tm=tm)
    return out.reshape(batch, seq, d)
```

```python
import functools
import math

import jax
import jax.numpy as jnp
from jax import lax
from jax.experimental import pallas as pl
from jax.experimental.pallas import tpu as pltpu

HEAD_DIM = 64
LANES = 128
HEADS_PER_BLOCK = LANES // HEAD_DIM
N_EXPERTS = 8
TOP_K = 2
RMS_EPS = 1e-6
LOG2_E = 1.4426950408889634
ATTN_TILE = 256
CUMSUM_CHUNK = 256
EXPERT_TILE = 512
DENSE_CHUNK = 256
EXPERT_CHUNK = 512
VMEM_LIMIT = 56 * 1024 * 1024

BF16 = jnp.bfloat16
F32 = jnp.float32


def _params(*semantics):
    return pltpu.CompilerParams(dimension_semantics=semantics, vmem_limit_bytes=VMEM_LIMIT)


def _dot(a, b):
    return jnp.dot(a, b, preferred_element_type=F32)


def _dot_nt(a, b):
    return lax.dot_general(a, b, (((1,), (1,)), ((), ())), preferred_element_type=F32)


def _rstd(x):
    return lax.rsqrt(jnp.mean(x * x, axis=-1, keepdims=True) + RMS_EPS)


def _log_sigmoid(u):
    return jnp.minimum(u, 0.0) - jnp.log1p(jnp.exp(-jnp.abs(u)))


def _split3(x):
    hi = x.astype(BF16)
    r1 = x - hi.astype(F32)
    mid = r1.astype(BF16)
    lo = (r1 - mid.astype(F32)).astype(BF16)
    return hi, mid, lo


def _resident(block_shape, index_map):
    return pl.BlockSpec(block_shape, index_map, pipeline_mode=pl.Buffered(1))


def _qkvf_kernel(x_ref, g_ref, w_ref, wf_ref, bf_ref, qkv_ref, lf_ref):
    d = x_ref.shape[1]
    x = x_ref[...]
    xn = (x * _rstd(x) * g_ref[...]).astype(BF16)
    f = _dot_nt(wf_ref[...], xn) + bf_ref[...]
    lf_ref[...] = _log_sigmoid(f)
    for part in range(3):
        cols = slice(part * d, (part + 1) * d)
        y = _dot(xn, w_ref[:, cols])
        if part == 0:
            y = y * (1.0 / math.sqrt(HEAD_DIM))
        qkv_ref[:, cols] = y.astype(BF16)


def _qkvf_proj(x, g, w_in, w_f_t, b_f, *, tm=1024):
    t, d = x.shape
    n = 3 * d
    h = w_f_t.shape[0]
    const = lambda i: (0, 0)
    return pl.pallas_call(
        _qkvf_kernel,
        grid=(t // tm,),
        in_specs=[
            pl.BlockSpec((tm, d), lambda i: (i, 0)),
            _resident((1, d), const),
            _resident((d, n), const),
            _resident((h, d), const),
            _resident((h, 1), const),
        ],
        out_specs=[
            pl.BlockSpec((tm, n), lambda i: (i, 0)),
            pl.BlockSpec((h, tm), lambda i: (0, i)),
        ],
        out_shape=[
            jax.ShapeDtypeStruct((t, n), BF16),
            jax.ShapeDtypeStruct((h, t), F32),
        ],
        compiler_params=_params("parallel"),
        name="qkvf_proj",
    )(x, g, w_in, w_f_t, b_f)


def _cumsum_kernel(lf_ref, tri_ref, c_ref):
    h, s = lf_ref.shape
    tri = tri_ref[...]
    carry = jnp.zeros((h, 1), F32)
    for c0 in range(0, s, CUMSUM_CHUNK):
        hi, mid, lo = _split3(lf_ref[:, c0:c0 + CUMSUM_CHUNK])
        cs = _dot(lo, tri) + _dot(mid, tri) + _dot(hi, tri) + carry
        for hd in range(h):
            c_ref[hd, :, c0:c0 + CUMSUM_CHUNK] = cs[hd:hd + 1, :]
        carry = cs[:, CUMSUM_CHUNK - 1:CUMSUM_CHUNK]


def _forget_cumsum(lf_t, tri, *, seq):
    h, t = lf_t.shape
    return pl.pallas_call(
        _cumsum_kernel,
        grid=(t // seq,),
        in_specs=[
            pl.BlockSpec((h, seq), lambda b: (0, b)),
            pl.BlockSpec((CUMSUM_CHUNK, CUMSUM_CHUNK), lambda b: (0, 0)),
        ],
        out_specs=pl.BlockSpec((h, 1, seq), lambda b: (0, 0, b)),
        out_shape=jax.ShapeDtypeStruct((h, 1, t), F32),
        compiler_params=_params("parallel"),
        name="forget_cumsum",
    )(lf_t, tri)


def _head_masks(shape):
    lane = lax.broadcasted_iota(jnp.int32, shape, 1)
    return lane < HEAD_DIM


def _split_heads(x2, first_head_lanes, fill):
    return (jnp.where(first_head_lanes, x2, fill), jnp.where(first_head_lanes, fill, x2))


def _stack_heads(q2, first_head_lanes):
    return jnp.concatenate(_split_heads(q2, first_head_lanes, jnp.zeros_like(q2)), axis=0)


def _lane_tile(x, width):
    return jnp.concatenate([x] * (width // LANES), axis=1)


def _fox_kernel(q_ref, k_ref, v_ref, c_ref, o_ref, *scratch):
    tq = ATTN_TILE
    nq = q_ref.shape[0] // tq
    m_refs, acc_refs = scratch[:nq], scratch[nq:]
    first = _head_masks((tq, LANES))

    def tile(qs, m_ref, acc_ref, kt, diagonal):
        k = k_ref[kt * tq:(kt + 1) * tq, :]
        v = v_ref[kt * tq:(kt + 1) * tq, :]
        v_one = jnp.concatenate([v, jnp.ones_like(v)], axis=1)
        s = _dot_nt(qs, k)
        s = jnp.concatenate([s[hd * tq:(hd + 1) * tq] - c_ref[hd, :, kt * tq:(kt + 1) * tq]
                             for hd in range(HEADS_PER_BLOCK)], axis=0)
        if diagonal:
            row = lax.broadcasted_iota(jnp.int32, s.shape, 0) & (tq - 1)
            col = lax.broadcasted_iota(jnp.int32, s.shape, 1)
            s = jnp.where(col <= row, s, -jnp.inf)
        m_prev = m_ref[...]
        m_new = jnp.maximum(m_prev, jnp.max(s, axis=1, keepdims=True))
        alpha = jnp.exp(m_prev - m_new)
        p = jnp.exp(s - _lane_tile(m_new, tq))
        acc_ref[...] = _lane_tile(alpha, 2 * LANES) * acc_ref[...] + _dot(p.astype(BF16), v_one)
        m_ref[...] = m_new

    for qt in range(nq):
        qs = _stack_heads(q_ref[qt * tq:(qt + 1) * tq, :], first)
        m_ref, acc_ref = m_refs[qt], acc_refs[qt]
        m_ref[...] = jnp.full(m_ref.shape, -jnp.inf, F32)
        acc_ref[...] = jnp.zeros(acc_ref.shape, F32)
        for kt in range(qt + 1):
            tile(qs, m_ref, acc_ref, kt, kt == qt)
        out = acc_ref[:, :LANES] / acc_ref[:, LANES:]
        o_ref[qt * tq:(qt + 1) * tq, :] = jnp.where(first, out[:tq], out[tq:]).astype(BF16)


def _fox_attention(qkv, c, *, batch, seq, d_model):
    t = qkv.shape[0]
    n_pairs = d_model // LANES
    tq = ATTN_TILE
    nq = seq // tq
    assert tq & (tq - 1) == 0, "the diagonal mask takes row % tq as row & (tq - 1)"
    rows = HEADS_PER_BLOCK * tq
    return pl.pallas_call(
        _fox_kernel,
        grid=(batch, n_pairs),
        in_specs=[
            pl.BlockSpec((seq, LANES), lambda b, hp: (b, hp)),
            pl.BlockSpec((seq, LANES), lambda b, hp: (b, n_pairs + hp)),
            pl.BlockSpec((seq, LANES), lambda b, hp: (b, 2 * n_pairs + hp)),
            pl.BlockSpec((HEADS_PER_BLOCK, 1, seq), lambda b, hp: (hp, 0, b)),
        ],
        out_specs=pl.BlockSpec((seq, LANES), lambda b, hp: (b, hp)),
        out_shape=jax.ShapeDtypeStruct((t, d_model), BF16),
        scratch_shapes=([pltpu.VMEM((rows, LANES), F32)] * nq + [pltpu.VMEM((rows, 2 * LANES), F32)] * nq),
        compiler_params=_params("parallel", "parallel"),
        name="fox_attention",
    )(qkv, qkv, qkv, c)


def _sb_kernel(q_ref, k_ref, v_ref, tri_ref, o_ref, *scratch):
    tq = ATTN_TILE
    nq = q_ref.shape[0] // tq
    r_refs, acc_refs = scratch[:nq], scratch[nq:]
    sub = CUMSUM_CHUNK
    n_sub = tq // sub
    first = _head_masks((tq, LANES))

    def tile(qs, r_ref, acc_ref, kt, diagonal):
        k = k_ref[kt * tq:(kt + 1) * tq, :]
        v = v_ref[kt * tq:(kt + 1) * tq, :]
        z = _dot_nt(qs, k)
        nlm = jnp.maximum(z, 0.0) + jnp.log(1.0 + jnp.exp2(jnp.abs(z) * (-LOG2_E)))
        if diagonal:
            row = lax.broadcasted_iota(jnp.int32, z.shape, 0) & (tq - 1)
            col = lax.broadcasted_iota(jnp.int32, z.shape, 1)
            visible = col < row
            nlm = jnp.where(visible, nlm, 0.0)
        nlm16 = nlm.astype(BF16)
        p = [_dot(nlm16[:, sb * sub:(sb + 1) * sub], tri_ref[...]) for sb in range(n_sub)]
        r = r_ref[...]
        later = [None] * n_sub
        for sb in reversed(range(n_sub)):
            later[sb] = p[sb] + _lane_tile(r, sub)
            r = r + p[sb][:, 0:1]
        w = jnp.exp(z - jnp.concatenate(later, axis=1))
        if diagonal:
            w = jnp.where(visible, w, 0.0)
        acc_ref[...] += _dot(w.astype(BF16), v)
        r_ref[...] = r

    for qt in range(nq):
        qs = _stack_heads(q_ref[qt * tq:(qt + 1) * tq, :], first)
        r_ref, acc_ref = r_refs[qt], acc_refs[qt]
        r_ref[...] = jnp.zeros(r_ref.shape, F32)
        acc_ref[...] = jnp.zeros(acc_ref.shape, F32)
        for kt in reversed(range(qt + 1)):
            tile(qs, r_ref, acc_ref, kt, kt == qt)
        o_ref[qt * tq:(qt + 1) * tq, :] = jnp.where(first, acc_ref[:tq], acc_ref[tq:]).astype(BF16)


def _sb_attention(q, kv, tri, *, batch, seq, d_model):
    t = q.shape[0]
    n_pairs = d_model // LANES
    tq = ATTN_TILE
    nq = seq // tq
    assert tq & (tq - 1) == 0, "the diagonal mask takes row % tq as row & (tq - 1)"
    stat = pltpu.VMEM((HEADS_PER_BLOCK * tq, LANES), F32)
    return pl.pallas_call(
        _sb_kernel,
        grid=(batch, n_pairs),
        in_specs=[
            pl.BlockSpec((seq, LANES), lambda b, hp: (b, hp)),
            pl.BlockSpec((seq, LANES), lambda b, hp: (b, hp)),
            pl.BlockSpec((seq, LANES), lambda b, hp: (b, n_pairs + hp)),
            pl.BlockSpec(tri.shape, lambda b, hp: (0, 0)),
        ],
        out_specs=pl.BlockSpec((seq, LANES), lambda b, hp: (b, hp)),
        out_shape=jax.ShapeDtypeStruct((t, d_model), BF16),
        scratch_shapes=[stat] * (2 * nq),
        compiler_params=_params("parallel", "parallel"),
        name="sb_attention",
    )(q, kv, kv, tri)


def _oproj_router_kernel(o_ref, w_ref, h_ref, g_ref, rw_ref, rb_ref, h1_ref, fn_ref, idx_ref, gate_ref):
    h1 = h_ref[...] + _dot(o_ref[...], w_ref[...])
    h1_ref[...] = h1
    fn = h1 * _rstd(h1) * g_ref[...]
    fn_ref[...] = fn
    fn_hi = fn.astype(BF16)
    fn_lo = (fn - fn_hi.astype(F32)).astype(BF16)
    lhs = jnp.concatenate([fn_hi, fn_lo, fn_hi], axis=1)
    half = lhs.shape[0] // 2
    logits = jnp.concatenate([_dot(lhs[:half], rw_ref[...]), _dot(lhs[half:], rw_ref[...])], axis=0) + rb_ref[...]
    lane = lax.broadcasted_iota(jnp.int32, logits.shape, 1)
    logits = jnp.where(lane < N_EXPERTS, logits, -jnp.inf)
    lane_f = lane.astype(F32)
    m1 = jnp.max(logits, axis=1, keepdims=True)
    i1 = jnp.min(jnp.where(logits == m1, lane_f, float(LANES)), axis=1, keepdims=True)
    rest = jnp.where(lane_f == i1, -jnp.inf, logits)
    m2 = jnp.max(rest, axis=1, keepdims=True)
    i2 = jnp.min(jnp.where(rest == m2, lane_f, float(LANES)), axis=1, keepdims=True)
    e2 = jnp.exp(m2 - m1)
    g1 = 1.0 / (1.0 + e2)
    g2 = e2 / (1.0 + e2)
    idx_ref[...] = jnp.where(lane == 0, i1, jnp.where(lane == 1, i2, 0.0)).astype(jnp.int32)
    gate_ref[...] = jnp.where(lane == 0, g1, jnp.where(lane == 1, g2, 0.0))


def _oproj_router(o, w_o, h, g, rw, rb, *, tm=1024):
    t, d = h.shape
    row = lambda i: (i, 0)
    const = lambda i: (0, 0)
    return pl.pallas_call(
        _oproj_router_kernel,
        grid=(t // tm,),
        in_specs=[pl.BlockSpec((tm, d), row), _resident((d, d), const), pl.BlockSpec((tm, d), row),
                  _resident((1, d), const), _resident(rw.shape, const), _resident((1, LANES), const)],
        out_specs=[pl.BlockSpec((tm, d), row), pl.BlockSpec((tm, d), row),
                   pl.BlockSpec((tm, LANES), row), pl.BlockSpec((tm, LANES), row)],
        out_shape=[jax.ShapeDtypeStruct((t, d), F32), jax.ShapeDtypeStruct((t, d), F32),
                   jax.ShapeDtypeStruct((t, LANES), jnp.int32), jax.ShapeDtypeStruct((t, LANES), F32)],
        compiler_params=_params("parallel"),
        name="oproj_norm_router",
    )(o, w_o, h, g, rw, rb)


def _swiglu_accumulate(x, w_gate_ref, w_up_ref, w_down_ref, out_ref, *, chunk, up_offset=0, before_chunk=None):
    f = w_down_ref.shape[0]
    for c0 in range(0, f, chunk):
        if before_chunk is not None:
            before_chunk(c0 // chunk)
        g = _dot(x, w_gate_ref[:, c0:c0 + chunk])
        u = _dot(x, w_up_ref[:, up_offset + c0:up_offset + c0 + chunk])
        act = (g * (1.0 / (1.0 + jnp.exp(-g))) * u).astype(BF16)
        out_ref[...] += _dot(act, w_down_ref[c0:c0 + chunk, :])


def _ple(h2, p, wp, wg):
    gate = 1.0 / (1.0 + jnp.exp(-_dot(h2.astype(BF16), wg)))
    return h2 + _dot(p.astype(BF16), wp) * gate


def _layer0_tail_kernel(o_ref, x_ref, p_ref, wo_ref, g_ref, wgu_ref, wd_ref, wp_ref, wg_ref, gains_ref,
                        wkv_ref, wq_ref, h3_ref, kv_ref, q_ref):
    h1 = x_ref[...] + _dot(o_ref[...], wo_ref[...])
    fn = (h1 * _rstd(h1) * g_ref[...]).astype(BF16)
    h3_ref[...] = h1
    _swiglu_accumulate(fn, wgu_ref, wgu_ref, wd_ref, h3_ref, chunk=DENSE_CHUNK, up_offset=wd_ref.shape[0])
    h3 = _ple(h3_ref[...], p_ref[...], wp_ref[...], wg_ref[...])
    h3_ref[...] = h3
    y = h3 * _rstd(h3)
    kv_ref[...] = _dot((y * gains_ref[0:1, :]).astype(BF16), wkv_ref[...]).astype(BF16)
    q = _dot((y * gains_ref[1:2, :]).astype(BF16), wq_ref[...])
    q_ref[...] = (q * (1.0 / math.sqrt(HEAD_DIM))).astype(BF16)


def _layer_rows(p, layer, tm):
    return pl.BlockSpec((None, tm, p.shape[2]), lambda i: (layer, i, 0))


def _layer0_tail(o, x, p, w_o, g, w_gu, w_down, wp, wg, gains, w_kv, w_q, *, tm=512):
    t, d = x.shape
    row = lambda i: (i, 0)
    const = lambda i: (0, 0)
    weights = (w_o, g, w_gu, w_down, wp, wg, gains, w_kv, w_q)
    return pl.pallas_call(
        _layer0_tail_kernel,
        grid=(t // tm,),
        in_specs=[pl.BlockSpec((tm, d), row), pl.BlockSpec((tm, d), row), _layer_rows(p, 0, tm)]
                 + [_resident(w.shape, const) for w in weights],
        out_specs=[pl.BlockSpec((tm, d), row), pl.BlockSpec((tm, w_kv.shape[1]), row), pl.BlockSpec((tm, d), row)],
        out_shape=[jax.ShapeDtypeStruct((t, d), F32), jax.ShapeDtypeStruct((t, w_kv.shape[1]), BF16),
                   jax.ShapeDtypeStruct((t, d), BF16)],
        compiler_params=_params("parallel"),
        name="layer0_tail",
    )(o, x, p, *weights)


def _dispatch_kernel(pend_ref, padded_ref, dest_ref, x_ref, xb_hbm, zero_ref, sem, zero_sem, *, tm):
    n_tiles = xb_hbm.shape[0] // EXPERT_TILE

    def zero_tile(tile):
        row0 = pl.multiple_of(tile * EXPERT_TILE, EXPERT_TILE)
        return pltpu.make_async_copy(zero_ref, xb_hbm.at[pl.ds(row0, EXPERT_TILE)], zero_sem)

    def last_tile(e):
        return pend_ref[e] // EXPERT_TILE - 1

    def for_unused_tiles(fn):
        def body(tile, carry):
            fn(tile)
            return carry
        lax.fori_loop(pend_ref[N_EXPERTS - 1] // EXPERT_TILE, n_tiles, body, 0)

    @pl.when(pl.program_id(0) == 0)
    def _():
        zero_ref[...] = jnp.zeros(zero_ref.shape, zero_ref.dtype)
        for e in range(N_EXPERTS):
            pl.when(padded_ref[e] > 0)(lambda e=e: zero_tile(last_tile(e)).start())
        for_unused_tiles(lambda tile: zero_tile(tile).start())
        for e in range(N_EXPERTS):
            pl.when(padded_ref[e] > 0)(lambda e=e: zero_tile(last_tile(e)).wait())
        for_unused_tiles(lambda tile: zero_tile(tile).wait())

    def issue(r, carry):
        for k in range(TOP_K):
            pltpu.make_async_copy(x_ref.at[pl.ds(r, 1)],
                                  xb_hbm.at[pl.ds(dest_ref[0, 0, r * TOP_K + k], 1)], sem).start()
        return carry

    lax.fori_loop(0, tm, issue, 0, unroll=8)
    for _ in range(TOP_K):
        pltpu.make_async_copy(x_ref, xb_hbm.at[pl.ds(0, tm)], sem).wait()


def _dispatch(x, dest3, pend, padded, n_rows, *, tm):
    t, d = x.shape
    return pl.pallas_call(
        functools.partial(_dispatch_kernel, tm=tm),
        grid_spec=pltpu.PrefetchScalarGridSpec(
            num_scalar_prefetch=2,
            grid=(t // tm,),
            in_specs=[
                pl.BlockSpec((1, 1, tm * TOP_K), lambda i, pe, pa: (i, 0, 0), memory_space=pltpu.SMEM),
                pl.BlockSpec((tm, d), lambda i, pe, pa: (i, 0)),
            ],
            out_specs=pl.BlockSpec(memory_space=pl.ANY),
            scratch_shapes=[pltpu.VMEM((EXPERT_TILE, d), x.dtype), pltpu.SemaphoreType.DMA,
                            pltpu.SemaphoreType.DMA],
        ),
        out_shape=jax.ShapeDtypeStruct((n_rows, d), x.dtype),
        compiler_params=_params("arbitrary"),
        name="moe_dispatch",
    )(pend, padded, dest3, x)


def _expert_kernel(te_ref, th_ref, x_ref, wgu_hbm, wd_hbm, y_ref, wg_ref, wu_ref, wd_ref,
                   sg_ref, su_ref, sd_ref, sems, *, chunk):
    t = pl.program_id(0)
    e = te_ref[t]
    halves = th_ref[t]
    half = EXPERT_TILE // 2
    fe = wd_ref.shape[0]
    n_chunks = fe // chunk
    y_ref[...] = jnp.zeros(y_ref.shape, F32)

    def chunk_copies(j, slot):
        cols = pl.ds(j * chunk, chunk)
        return (pltpu.make_async_copy(wgu_hbm.at[e, :, cols], sg_ref.at[slot], sems.at[0, slot]),
                pltpu.make_async_copy(wgu_hbm.at[e, :, pl.ds(fe + j * chunk, chunk)], su_ref.at[slot], sems.at[1, slot]),
                pltpu.make_async_copy(wd_hbm.at[e, cols, :], sd_ref.at[slot], sems.at[2, slot]))

    def load_chunk(j):
        slot = j % 2
        if j + 1 < n_chunks:
            for c in chunk_copies(j + 1, 1 - slot):
                c.start()
        for c in chunk_copies(j, slot):
            c.wait()
        wg_ref[:, j * chunk:(j + 1) * chunk] = sg_ref[slot].astype(BF16)
        wu_ref[:, j * chunk:(j + 1) * chunk] = su_ref[slot].astype(BF16)
        wd_ref[j * chunk:(j + 1) * chunk, :] = sd_ref[slot].astype(BF16)

    def run(x, out_ref, streaming):
        if streaming:
            for c in chunk_copies(0, 0):
                c.start()
        _swiglu_accumulate(x.astype(BF16), wg_ref, wu_ref, wd_ref, out_ref, chunk=chunk,
                           before_chunk=load_chunk if streaming else None)

    new_expert = jnp.logical_or(t == 0, te_ref[jnp.maximum(t - 1, 0)] != e)
    for streaming in (True, False):
        first = new_expert if streaming else jnp.logical_not(new_expert)
        pl.when(jnp.logical_and(first, halves == 2))(lambda s=streaming: run(x_ref[...], y_ref, s))
        pl.when(jnp.logical_and(first, halves == 1))(
            lambda s=streaming: run(x_ref[:half, :], y_ref.at[pl.ds(0, half)], s))


def _experts(xb, w_gu, w_down, tile_expert, tile_halves, *, chunk=EXPERT_CHUNK):
    n_rows, d = xb.shape
    fe = w_down.shape[1]
    n_tiles = n_rows // EXPERT_TILE
    return pl.pallas_call(
        functools.partial(_expert_kernel, chunk=chunk),
        grid_spec=pltpu.PrefetchScalarGridSpec(
            num_scalar_prefetch=2,
            grid=(n_tiles,),
            in_specs=[
                pl.BlockSpec((EXPERT_TILE, d), lambda ti, te, th: (ti, 0)),
                pl.BlockSpec(memory_space=pl.ANY),
                pl.BlockSpec(memory_space=pl.ANY),
            ],
            out_specs=pl.BlockSpec((EXPERT_TILE, d), lambda ti, te, th: (ti, 0)),
            scratch_shapes=[
                pltpu.VMEM((d, fe), BF16), pltpu.VMEM((d, fe), BF16), pltpu.VMEM((fe, d), BF16),
                pltpu.VMEM((2, d, chunk), F32), pltpu.VMEM((2, d, chunk), F32), pltpu.VMEM((2, chunk, d), F32),
                pltpu.SemaphoreType.DMA((3, 2)),
            ],
        ),
        out_shape=jax.ShapeDtypeStruct((n_rows, d), F32),
        compiler_params=_params("arbitrary"),
        name="moe_experts",
    )(tile_expert, tile_halves, xb, w_gu, w_down)


def _combine_kernel(dest_ref, dest_next_ref, yb_hbm, h_ref, gate_ref, p_ref, wp_ref, wg_ref, g_ref, o_ref,
                    y_ref, sems, *, tm):
    i = pl.program_id(0)
    slot = i % 2

    def gather(d_ref, s):
        def issue(r, carry):
            for k in range(TOP_K):
                pltpu.make_async_copy(yb_hbm.at[pl.ds(d_ref[0, 0, r * TOP_K + k], 1)],
                                      y_ref.at[s, k, pl.ds(r, 1)], sems.at[s]).start()
            return carry
        lax.fori_loop(0, tm, issue, 0, unroll=8)

    pl.when(i == 0)(lambda: gather(dest_ref, 0))
    pl.when(i + 1 < pl.num_programs(0))(lambda: gather(dest_next_ref, 1 - slot))
    for k in range(TOP_K):
        pltpu.make_async_copy(yb_hbm.at[pl.ds(0, tm)], y_ref.at[slot, k], sems.at[slot]).wait()

    gates = gate_ref[...]
    h2 = h_ref[...] + (y_ref[slot, 0] * gates[:, 0:1] + y_ref[slot, 1] * gates[:, 1:2])
    h3 = _ple(h2, p_ref[...], wp_ref[...], wg_ref[...])
    o_ref[...] = h3 * _rstd(h3) * g_ref[...]


def _combine(dest3, yb, h, gates, p, wp, wg, g, *, tm):
    t, d = h.shape
    dp = p.shape[2]
    n = t // tm
    row = lambda i: (i, 0)
    const = lambda i: (0, 0)
    dest_spec = lambda index_map: pl.BlockSpec((1, 1, tm * TOP_K), index_map, memory_space=pltpu.SMEM)
    return pl.pallas_call(
        functools.partial(_combine_kernel, tm=tm),
        grid=(n,),
        in_specs=[
            dest_spec(lambda i: (i, 0, 0)),
            dest_spec(lambda i: (jnp.minimum(i + 1, n - 1), 0, 0)),
            pl.BlockSpec(memory_space=pl.ANY),
            pl.BlockSpec((tm, d), row),
            pl.BlockSpec((tm, LANES), row),
            _layer_rows(p, 1, tm),
            _resident((dp, d), const),
            _resident((d, d), const),
            _resident((1, d), const),
        ],
        out_specs=pl.BlockSpec((tm, d), row),
        out_shape=jax.ShapeDtypeStruct((t, d), F32),
        scratch_shapes=[pltpu.VMEM((2, TOP_K, tm, d), F32), pltpu.SemaphoreType.DMA((2,))],
        compiler_params=_params("arbitrary"),
        name="moe_combine",
    )(dest3, dest3, yb, h, gates, p, wp, wg, g)


def _routing_tables(idx, n_tokens, *, tm):
    e_flat = idx.reshape(-1)
    onehot = (e_flat[:, None] == jnp.arange(N_EXPERTS, dtype=jnp.int32)[None, :]).astype(jnp.int32)
    csum = jnp.cumsum(onehot, axis=0)
    counts = csum[-1]
    padded = (counts + EXPERT_TILE - 1) // EXPERT_TILE * EXPERT_TILE
    pend = jnp.cumsum(padded)
    pstart = pend - padded
    dest = jnp.sum(onehot * (csum - 1 + pstart[None, :]), axis=1)
    n_tiles = (n_tokens * TOP_K) // EXPERT_TILE + N_EXPERTS
    tile_row0 = jnp.arange(n_tiles, dtype=jnp.int32) * EXPERT_TILE
    tile_expert = jnp.minimum(jnp.sum((tile_row0[:, None] >= pend[None, :]).astype(jnp.int32), axis=1),
                              N_EXPERTS - 1)
    tile_rows = jnp.clip(counts[tile_expert] - (tile_row0 - pstart[tile_expert]), 0, EXPERT_TILE)
    tile_halves = ((tile_rows + EXPERT_TILE // 2 - 1) // (EXPERT_TILE // 2)).astype(jnp.int32)
    dest3 = dest.astype(jnp.int32).reshape(n_tokens // tm, 1, tm * TOP_K)
    return dest3, tile_expert, tile_halves, pend.astype(jnp.int32), padded.astype(jnp.int32), n_tiles


def kernel(x, p, attn_norm, ffn_norm, w_in_a, b_f, w_o_a, kv_norm, w_kv, w_q_b, w_o_b, w_gu_dense, w_down_dense, router_w, router_b, w_gu_moe, w_down_moe, w_ple_proj, w_ple_gate, final_norm):
    batch, seq, d = x.shape
    t = batch * seq
    n_heads = d // HEAD_DIM
    x2 = x.reshape(t, d)
    p2 = p.reshape(p.shape[0], t, p.shape[-1])
    bf = lambda w: w.astype(BF16)

    ii = lax.broadcasted_iota(jnp.int32, (CUMSUM_CHUNK, CUMSUM_CHUNK), 0)
    jj = lax.broadcasted_iota(jnp.int32, (CUMSUM_CHUNK, CUMSUM_CHUNK), 1)
    tri_le = (ii <= jj).astype(BF16)
    tri_ge = (ii >= jj).astype(BF16)

    w_in = bf(w_in_a[0])
    qkv, lf_t = _qkvf_proj(x2, attn_norm[0:1], w_in, w_in[:, 3 * d:].T, b_f[0].reshape(n_heads, 1))
    c = _forget_cumsum(lf_t, tri_le, seq=seq)
    o0 = _fox_attention(qkv, c, batch=batch, seq=seq, d_model=d)
    gains = jnp.stack([kv_norm, attn_norm[1]])
    h3, kv, q1 = _layer0_tail(o0, x2, p2, bf(w_o_a[0]), ffn_norm[0:1], bf(w_gu_dense[0]), bf(w_down_dense[0]),
                              bf(w_ple_proj[0]), bf(w_ple_gate[0]), gains, bf(w_kv), bf(w_q_b[0]))

    o1 = _sb_attention(q1, kv, tri_ge, batch=batch, seq=seq, d_model=d)
    rw = jnp.pad(router_w[0], ((0, 0), (0, LANES - N_EXPERTS)))
    rw_hi = bf(rw)
    rw_lo = bf(rw - rw_hi.astype(F32))
    rw = jnp.concatenate([rw_hi, rw_hi, rw_lo], axis=0)
    rb = jnp.pad(router_b[0], (0, LANES - N_EXPERTS)).reshape(1, LANES)
    h4, fn1, idx, gates = _oproj_router(o1, bf(w_o_b[0]), h3, ffn_norm[1:2], rw, rb)

    tm = 1024
    dest3, tile_expert, tile_halves, pend, padded, n_tiles = _routing_tables(idx[:, :TOP_K], t, tm=tm)
    xb = _dispatch(fn1, dest3, pend, padded, n_tiles * EXPERT_TILE, tm=tm)
    yb = _experts(xb, w_gu_moe[0], w_down_moe[0], tile_expert, tile_halves)
    out = _combine(dest3, yb, h4, gates, p2, bf(w_ple_proj[1]), bf(w_ple_gate[1]),
                   final_norm.reshape(1, d), tm=tm)
    return out.reshape(batch, seq, d)
```

```python
import functools
import math

import jax
import jax.numpy as jnp
from jax import lax
from jax.experimental import pallas as pl
from jax.experimental.pallas import tpu as pltpu

HEAD_DIM = 64
LANES = 128
HEADS_PER_BLOCK = LANES // HEAD_DIM
N_EXPERTS = 8
TOP_K = 2
RMS_EPS = 1e-6
LOG2_E = 1.4426950408889634
ATTN_TILE = 256
CUMSUM_CHUNK = 256
EXPERT_TILE = 512
DENSE_CHUNK = 256
EXPERT_CHUNK = 512
VMEM_LIMIT = 56 * 1024 * 1024

BF16 = jnp.bfloat16
F32 = jnp.float32


def _params(*semantics):
    return pltpu.CompilerParams(dimension_semantics=semantics, vmem_limit_bytes=VMEM_LIMIT)


def _dot(a, b):
    return jnp.dot(a, b, preferred_element_type=F32)


def _dot_nt(a, b):
    return lax.dot_general(a, b, (((1,), (1,)), ((), ())), preferred_element_type=F32)


def _rstd(x):
    return lax.rsqrt(jnp.mean(x * x, axis=-1, keepdims=True) + RMS_EPS)


def _log_sigmoid(u):
    return jnp.minimum(u, 0.0) - jnp.log1p(jnp.exp(-jnp.abs(u)))


def _split3(x):
    hi = x.astype(BF16)
    r1 = x - hi.astype(F32)
    mid = r1.astype(BF16)
    lo = (r1 - mid.astype(F32)).astype(BF16)
    return hi, mid, lo


def _resident(block_shape, index_map):
    return pl.BlockSpec(block_shape, index_map, pipeline_mode=pl.Buffered(1))


def _qkvf_kernel(x_ref, g_ref, w_ref, wf_ref, bf_ref, tri_ref, qkv_ref, c_ref, carry_ref, *, tiles_per_seq):
    d = x_ref.shape[1]
    x = x_ref[...]
    xn = (x * _rstd(x) * g_ref[...]).astype(BF16)
    f = _dot_nt(wf_ref[...], xn) + bf_ref[...]
    lf = _log_sigmoid(f)

    @pl.when(pl.program_id(0) % tiles_per_seq == 0)
    def _():
        carry_ref[...] = jnp.zeros(carry_ref.shape, F32)

    carry = carry_ref[...]
    for c0 in range(0, lf.shape[1], CUMSUM_CHUNK):
        hi, mid, lo = _split3(lf[:, c0:c0 + CUMSUM_CHUNK])
        cs = _dot(lo, tri_ref[...]) + _dot(mid, tri_ref[...]) + _dot(hi, tri_ref[...]) + carry
        for hd in range(lf.shape[0]):
            c_ref[hd, :, c0:c0 + CUMSUM_CHUNK] = cs[hd:hd + 1, :]
        carry = cs[:, CUMSUM_CHUNK - 1:CUMSUM_CHUNK]
    carry_ref[...] = carry

    for part in range(3):
        cols = slice(part * d, (part + 1) * d)
        y = _dot(xn, w_ref[:, cols])
        if part == 0:
            y = y * (1.0 / math.sqrt(HEAD_DIM))
        qkv_ref[:, cols] = y.astype(BF16)


def _qkvf_proj(x, g, w_in, w_f_t, b_f, tri, *, seq, tm=1024):
    t, d = x.shape
    n = 3 * d
    h = w_f_t.shape[0]
    const = lambda i: (0, 0)
    return pl.pallas_call(
        functools.partial(_qkvf_kernel, tiles_per_seq=seq // tm),
        grid=(t // tm,),
        in_specs=[
            pl.BlockSpec((tm, d), lambda i: (i, 0)),
            _resident((1, d), const),
            _resident((d, n), const),
            _resident((h, d), const),
            _resident((h, 1), const),
            _resident(tri.shape, const),
        ],
        out_specs=[
            pl.BlockSpec((tm, n), lambda i: (i, 0)),
            pl.BlockSpec((h, 1, tm), lambda i: (0, 0, i)),
        ],
        out_shape=[
            jax.ShapeDtypeStruct((t, n), BF16),
            jax.ShapeDtypeStruct((h, 1, t), F32),
        ],
        scratch_shapes=[pltpu.VMEM((h, 1), F32)],
        compiler_params=_params("arbitrary"),
        name="qkvf_proj",
    )(x, g, w_in, w_f_t, b_f, tri)


def _head_masks(shape):
    lane = lax.broadcasted_iota(jnp.int32, shape, 1)
    return lane < HEAD_DIM


def _split_heads(x2, first_head_lanes, fill):
    return (jnp.where(first_head_lanes, x2, fill), jnp.where(first_head_lanes, fill, x2))


def _stack_heads(q2, first_head_lanes):
    return jnp.concatenate(_split_heads(q2, first_head_lanes, jnp.zeros_like(q2)), axis=0)


def _lane_tile(x, width):
    return jnp.concatenate([x] * (width // LANES), axis=1)


def _fox_kernel(q_ref, k_ref, v_ref, c_ref, o_ref, *scratch):
    tq = ATTN_TILE
    nq = q_ref.shape[0] // tq
    m_refs, acc_refs = scratch[:nq], scratch[nq:]
    first = _head_masks((tq, LANES))

    def tile(qs, m_ref, acc_ref, kt, diagonal):
        k = k_ref[kt * tq:(kt + 1) * tq, :]
        v = v_ref[kt * tq:(kt + 1) * tq, :]
        v_one = jnp.concatenate([v, jnp.ones_like(v)], axis=1)
        s = _dot_nt(qs, k)
        s = jnp.concatenate([s[hd * tq:(hd + 1) * tq] - c_ref[hd, :, kt * tq:(kt + 1) * tq]
                             for hd in range(HEADS_PER_BLOCK)], axis=0)
        if diagonal:
            row = lax.broadcasted_iota(jnp.int32, s.shape, 0) & (tq - 1)
            col = lax.broadcasted_iota(jnp.int32, s.shape, 1)
            s = jnp.where(col <= row, s, -jnp.inf)
        m_prev = m_ref[...]
        m_new = jnp.maximum(m_prev, jnp.max(s, axis=1, keepdims=True))
        alpha = jnp.exp(m_prev - m_new)
        p = jnp.exp(s - _lane_tile(m_new, tq))
        acc_ref[...] = _lane_tile(alpha, 2 * LANES) * acc_ref[...] + _dot(p.astype(BF16), v_one)
        m_ref[...] = m_new

    for qt in range(nq):
        qs = _stack_heads(q_ref[qt * tq:(qt + 1) * tq, :], first)
        m_ref, acc_ref = m_refs[qt], acc_refs[qt]
        m_ref[...] = jnp.full(m_ref.shape, -jnp.inf, F32)
        acc_ref[...] = jnp.zeros(acc_ref.shape, F32)
        for kt in range(qt + 1):
            tile(qs, m_ref, acc_ref, kt, kt == qt)
        out = acc_ref[:, :LANES] / acc_ref[:, LANES:]
        o_ref[qt * tq:(qt + 1) * tq, :] = jnp.where(first, out[:tq], out[tq:]).astype(BF16)


def _fox_attention(qkv, c, *, batch, seq, d_model):
    t = qkv.shape[0]
    n_pairs = d_model // LANES
    tq = ATTN_TILE
    nq = seq // tq
    assert tq & (tq - 1) == 0, "the diagonal mask takes row % tq as row & (tq - 1)"
    rows = HEADS_PER_BLOCK * tq
    return pl.pallas_call(
        _fox_kernel,
        grid=(batch, n_pairs),
        in_specs=[
            pl.BlockSpec((seq, LANES), lambda b, hp: (b, hp)),
            pl.BlockSpec((seq, LANES), lambda b, hp: (b, n_pairs + hp)),
            pl.BlockSpec((seq, LANES), lambda b, hp: (b, 2 * n_pairs + hp)),
            pl.BlockSpec((HEADS_PER_BLOCK, 1, seq), lambda b, hp: (hp, 0, b)),
        ],
        out_specs=pl.BlockSpec((seq, LANES), lambda b, hp: (b, hp)),
        out_shape=jax.ShapeDtypeStruct((t, d_model), BF16),
        scratch_shapes=([pltpu.VMEM((rows, LANES), F32)] * nq + [pltpu.VMEM((rows, 2 * LANES), F32)] * nq),
        compiler_params=_params("parallel", "parallel"),
        name="fox_attention",
    )(qkv, qkv, qkv, c)


def _sb_kernel(q_ref, k_ref, v_ref, tri_ref, o_ref, *scratch):
    tq = ATTN_TILE
    nq = q_ref.shape[0] // tq
    r_refs, acc_refs = scratch[:nq], scratch[nq:]
    sub = CUMSUM_CHUNK
    n_sub = tq // sub
    first = _head_masks((tq, LANES))

    def tile(qs, r_ref, acc_ref, kt, diagonal):
        k = k_ref[kt * tq:(kt + 1) * tq, :]
        v = v_ref[kt * tq:(kt + 1) * tq, :]
        z = _dot_nt(qs, k)
        nlm = jnp.maximum(z, 0.0) + jnp.log(1.0 + jnp.exp2(jnp.abs(z) * (-LOG2_E)))
        if diagonal:
            row = lax.broadcasted_iota(jnp.int32, z.shape, 0) & (tq - 1)
            col = lax.broadcasted_iota(jnp.int32, z.shape, 1)
            visible = col < row
            nlm = jnp.where(visible, nlm, 0.0)
        nlm16 = nlm.astype(BF16)
        p = [_dot(nlm16[:, sb * sub:(sb + 1) * sub], tri_ref[...]) for sb in range(n_sub)]
        r = r_ref[...]
        later = [None] * n_sub
        for sb in reversed(range(n_sub)):
            later[sb] = p[sb] + _lane_tile(r, sub)
            r = r + p[sb][:, 0:1]
        w = jnp.exp(z - jnp.concatenate(later, axis=1))
        if diagonal:
            w = jnp.where(visible, w, 0.0)
        acc_ref[...] += _dot(w.astype(BF16), v)
        r_ref[...] = r

    for qt in range(nq):
        qs = _stack_heads(q_ref[qt * tq:(qt + 1) * tq, :], first)
        r_ref, acc_ref = r_refs[qt], acc_refs[qt]
        r_ref[...] = jnp.zeros(r_ref.shape, F32)
        acc_ref[...] = jnp.zeros(acc_ref.shape, F32)
        for kt in reversed(range(qt + 1)):
            tile(qs, r_ref, acc_ref, kt, kt == qt)
        o_ref[qt * tq:(qt + 1) * tq, :] = jnp.where(first, acc_ref[:tq], acc_ref[tq:]).astype(BF16)


def _sb_attention(q, kv, tri, *, batch, seq, d_model):
    t = q.shape[0]
    n_pairs = d_model // LANES
    tq = ATTN_TILE
    nq = seq // tq
    assert tq & (tq - 1) == 0, "the diagonal mask takes row % tq as row & (tq - 1)"
    stat = pltpu.VMEM((HEADS_PER_BLOCK * tq, LANES), F32)
    return pl.pallas_call(
        _sb_kernel,
        grid=(batch, n_pairs),
        in_specs=[
            pl.BlockSpec((seq, LANES), lambda b, hp: (b, hp)),
            pl.BlockSpec((seq, LANES), lambda b, hp: (b, hp)),
            pl.BlockSpec((seq, LANES), lambda b, hp: (b, n_pairs + hp)),
            pl.BlockSpec(tri.shape, lambda b, hp: (0, 0)),
        ],
        out_specs=pl.BlockSpec((seq, LANES), lambda b, hp: (b, hp)),
        out_shape=jax.ShapeDtypeStruct((t, d_model), BF16),
        scratch_shapes=[stat] * (2 * nq),
        compiler_params=_params("parallel", "parallel"),
        name="sb_attention",
    )(q, kv, kv, tri)


def _oproj_router_kernel(o_ref, w_ref, h_ref, g_ref, rw_ref, rb_ref, h1_ref, fn_ref, idx_ref, gate_ref):
    h1 = h_ref[...] + _dot(o_ref[...], w_ref[...])
    h1_ref[...] = h1
    fn = h1 * _rstd(h1) * g_ref[...]
    fn_ref[...] = fn
    fn_hi = fn.astype(BF16)
    fn_lo = (fn - fn_hi.astype(F32)).astype(BF16)
    lhs = jnp.concatenate([fn_hi, fn_lo, fn_hi], axis=1)
    half = lhs.shape[0] // 2
    logits = jnp.concatenate([_dot(lhs[:half], rw_ref[...]), _dot(lhs[half:], rw_ref[...])], axis=0) + rb_ref[...]
    lane = lax.broadcasted_iota(jnp.int32, logits.shape, 1)
    logits = jnp.where(lane < N_EXPERTS, logits, -jnp.inf)
    lane_f = lane.astype(F32)
    m1 = jnp.max(logits, axis=1, keepdims=True)
    i1 = jnp.min(jnp.where(logits == m1, lane_f, float(LANES)), axis=1, keepdims=True)
    rest = jnp.where(lane_f == i1, -jnp.inf, logits)
    m2 = jnp.max(rest, axis=1, keepdims=True)
    i2 = jnp.min(jnp.where(rest == m2, lane_f, float(LANES)), axis=1, keepdims=True)
    e2 = jnp.exp(m2 - m1)
    g1 = 1.0 / (1.0 + e2)
    g2 = e2 / (1.0 + e2)
    idx_ref[...] = jnp.where(lane == 0, i1, jnp.where(lane == 1, i2, 0.0)).astype(jnp.int32)
    gate_ref[...] = jnp.where(lane == 0, g1, jnp.where(lane == 1, g2, 0.0))


def _oproj_router(o, w_o, h, g, rw, rb, *, tm=1024):
    t, d = h.shape
    row = lambda i: (i, 0)
    const = lambda i: (0, 0)
    return pl.pallas_call(
        _oproj_router_kernel,
        grid=(t // tm,),
        in_specs=[pl.BlockSpec((tm, d), row), _resident((d, d), const), pl.BlockSpec((tm, d), row),
                  _resident((1, d), const), _resident(rw.shape, const), _resident((1, LANES), const)],
        out_specs=[pl.BlockSpec((tm, d), row), pl.BlockSpec((tm, d), row),
                   pl.BlockSpec((tm, LANES), row), pl.BlockSpec((tm, LANES), row)],
        out_shape=[jax.ShapeDtypeStruct((t, d), F32), jax.ShapeDtypeStruct((t, d), F32),
                   jax.ShapeDtypeStruct((t, LANES), jnp.int32), jax.ShapeDtypeStruct((t, LANES), F32)],
        compiler_params=_params("parallel"),
        name="oproj_norm_router",
    )(o, w_o, h, g, rw, rb)


def _swiglu_accumulate(x, w_gate_ref, w_up_ref, w_down_ref, out_ref, *, chunk, up_offset=0, before_chunk=None):
    f = w_down_ref.shape[0]
    for c0 in range(0, f, chunk):
        if before_chunk is not None:
            before_chunk(c0 // chunk)
        g = _dot(x, w_gate_ref[:, c0:c0 + chunk])
        u = _dot(x, w_up_ref[:, up_offset + c0:up_offset + c0 + chunk])
        act = (g * (1.0 / (1.0 + jnp.exp(-g))) * u).astype(BF16)
        out_ref[...] += _dot(act, w_down_ref[c0:c0 + chunk, :])


def _ple(h2, p, wp, wg):
    gate = 1.0 / (1.0 + jnp.exp(-_dot(h2.astype(BF16), wg)))
    return h2 + _dot(p.astype(BF16), wp) * gate


def _layer0_tail_kernel(o_ref, x_ref, p_ref, wo_ref, g_ref, wgu_ref, wd_ref, wp_ref, wg_ref, gains_ref,
                        wkv_ref, wq_ref, h3_ref, kv_ref, q_ref):
    h1 = x_ref[...] + _dot(o_ref[...], wo_ref[...])
    fn = (h1 * _rstd(h1) * g_ref[...]).astype(BF16)
    h3_ref[...] = h1
    _swiglu_accumulate(fn, wgu_ref, wgu_ref, wd_ref, h3_ref, chunk=DENSE_CHUNK, up_offset=wd_ref.shape[0])
    h3 = _ple(h3_ref[...], p_ref[...], wp_ref[...], wg_ref[...])
    h3_ref[...] = h3
    y = h3 * _rstd(h3)
    kv_ref[...] = _dot((y * gains_ref[0:1, :]).astype(BF16), wkv_ref[...]).astype(BF16)
    q = _dot((y * gains_ref[1:2, :]).astype(BF16), wq_ref[...])
    q_ref[...] = (q * (1.0 / math.sqrt(HEAD_DIM))).astype(BF16)


def _layer_rows(p, layer, tm):
    return pl.BlockSpec((None, tm, p.shape[2]), lambda i: (layer, i, 0))


def _layer0_tail(o, x, p, w_o, g, w_gu, w_down, wp, wg, gains, w_kv, w_q, *, tm=512):
    t, d = x.shape
    row = lambda i: (i, 0)
    const = lambda i: (0, 0)
    weights = (w_o, g, w_gu, w_down, wp, wg, gains, w_kv, w_q)
    return pl.pallas_call(
        _layer0_tail_kernel,
        grid=(t // tm,),
        in_specs=[pl.BlockSpec((tm, d), row), pl.BlockSpec((tm, d), row), _layer_rows(p, 0, tm)]
                 + [_resident(w.shape, const) for w in weights],
        out_specs=[pl.BlockSpec((tm, d), row), pl.BlockSpec((tm, w_kv.shape[1]), row), pl.BlockSpec((tm, d), row)],
        out_shape=[jax.ShapeDtypeStruct((t, d), F32), jax.ShapeDtypeStruct((t, w_kv.shape[1]), BF16),
                   jax.ShapeDtypeStruct((t, d), BF16)],
        compiler_params=_params("parallel"),
        name="layer0_tail",
    )(o, x, p, *weights)


def _dispatch_kernel(pend_ref, padded_ref, dest_ref, x_ref, xb_hbm, zero_ref, sem, zero_sem, *, tm):
    n_tiles = xb_hbm.shape[0] // EXPERT_TILE

    def zero_tile(tile):
        row0 = pl.multiple_of(tile * EXPERT_TILE, EXPERT_TILE)
        return pltpu.make_async_copy(zero_ref, xb_hbm.at[pl.ds(row0, EXPERT_TILE)], zero_sem)

    def last_tile(e):
        return pend_ref[e] // EXPERT_TILE - 1

    def for_unused_tiles(fn):
        def body(tile, carry):
            fn(tile)
            return carry
        lax.fori_loop(pend_ref[N_EXPERTS - 1] // EXPERT_TILE, n_tiles, body, 0)

    @pl.when(pl.program_id(0) == 0)
    def _():
        zero_ref[...] = jnp.zeros(zero_ref.shape, zero_ref.dtype)
        for e in range(N_EXPERTS):
            pl.when(padded_ref[e] > 0)(lambda e=e: zero_tile(last_tile(e)).start())
        for_unused_tiles(lambda tile: zero_tile(tile).start())
        for e in range(N_EXPERTS):
            pl.when(padded_ref[e] > 0)(lambda e=e: zero_tile(last_tile(e)).wait())
        for_unused_tiles(lambda tile: zero_tile(tile).wait())

    def issue(r, carry):
        for k in range(TOP_K):
            pltpu.make_async_copy(x_ref.at[pl.ds(r, 1)],
                                  xb_hbm.at[pl.ds(dest_ref[0, 0, r * TOP_K + k], 1)], sem).start()
        return carry

    lax.fori_loop(0, tm, issue, 0, unroll=8)
    for _ in range(TOP_K):
        pltpu.make_async_copy(x_ref, xb_hbm.at[pl.ds(0, tm)], sem).wait()


def _dispatch(x, dest3, pend, padded, n_rows, *, tm):
    t, d = x.shape
    return pl.pallas_call(
        functools.partial(_dispatch_kernel, tm=tm),
        grid_spec=pltpu.PrefetchScalarGridSpec(
            num_scalar_prefetch=2,
            grid=(t // tm,),
            in_specs=[
                pl.BlockSpec((1, 1, tm * TOP_K), lambda i, pe, pa: (i, 0, 0), memory_space=pltpu.SMEM),
                pl.BlockSpec((tm, d), lambda i, pe, pa: (i, 0)),
            ],
            out_specs=pl.BlockSpec(memory_space=pl.ANY),
            scratch_shapes=[pltpu.VMEM((EXPERT_TILE, d), x.dtype), pltpu.SemaphoreType.DMA,
                            pltpu.SemaphoreType.DMA],
        ),
        out_shape=jax.ShapeDtypeStruct((n_rows, d), x.dtype),
        compiler_params=_params("arbitrary"),
        name="moe_dispatch",
    )(pend, padded, dest3, x)


def _expert_kernel(te_ref, th_ref, x_ref, wgu_hbm, wd_hbm, y_ref, wg_ref, wu_ref, wd_ref,
                   sg_ref, su_ref, sd_ref, sems, *, chunk):
    t = pl.program_id(0)
    e = te_ref[t]
    halves = th_ref[t]
    half = EXPERT_TILE // 2
    fe = wd_ref.shape[0]
    n_chunks = fe // chunk
    y_ref[...] = jnp.zeros(y_ref.shape, F32)

    def chunk_copies(j, slot):
        cols = pl.ds(j * chunk, chunk)
        return (pltpu.make_async_copy(wgu_hbm.at[e, :, cols], sg_ref.at[slot], sems.at[0, slot]),
                pltpu.make_async_copy(wgu_hbm.at[e, :, pl.ds(fe + j * chunk, chunk)], su_ref.at[slot], sems.at[1, slot]),
                pltpu.make_async_copy(wd_hbm.at[e, cols, :], sd_ref.at[slot], sems.at[2, slot]))

    def load_chunk(j):
        slot = j % 2
        if j + 1 < n_chunks:
            for c in chunk_copies(j + 1, 1 - slot):
                c.start()
        for c in chunk_copies(j, slot):
            c.wait()
        wg_ref[:, j * chunk:(j + 1) * chunk] = sg_ref[slot].astype(BF16)
        wu_ref[:, j * chunk:(j + 1) * chunk] = su_ref[slot].astype(BF16)
        wd_ref[j * chunk:(j + 1) * chunk, :] = sd_ref[slot].astype(BF16)

    def run(x, out_ref, streaming):
        if streaming:
            for c in chunk_copies(0, 0):
                c.start()
        _swiglu_accumulate(x.astype(BF16), wg_ref, wu_ref, wd_ref, out_ref, chunk=chunk,
                           before_chunk=load_chunk if streaming else None)

    new_expert = jnp.logical_or(t == 0, te_ref[jnp.maximum(t - 1, 0)] != e)
    for streaming in (True, False):
        first = new_expert if streaming else jnp.logical_not(new_expert)
        pl.when(jnp.logical_and(first, halves == 2))(lambda s=streaming: run(x_ref[...], y_ref, s))
        pl.when(jnp.logical_and(first, halves == 1))(
            lambda s=streaming: run(x_ref[:half, :], y_ref.at[pl.ds(0, half)], s))


def _experts(xb, w_gu, w_down, tile_expert, tile_halves, *, chunk=EXPERT_CHUNK):
    n_rows, d = xb.shape
    fe = w_down.shape[1]
    n_tiles = n_rows // EXPERT_TILE
    return pl.pallas_call(
        functools.partial(_expert_kernel, chunk=chunk),
        grid_spec=pltpu.PrefetchScalarGridSpec(
            num_scalar_prefetch=2,
            grid=(n_tiles,),
            in_specs=[
                pl.BlockSpec((EXPERT_TILE, d), lambda ti, te, th: (ti, 0)),
                pl.BlockSpec(memory_space=pl.ANY),
                pl.BlockSpec(memory_space=pl.ANY),
            ],
            out_specs=pl.BlockSpec((EXPERT_TILE, d), lambda ti, te, th: (ti, 0)),
            scratch_shapes=[
                pltpu.VMEM((d, fe), BF16), pltpu.VMEM((d, fe), BF16), pltpu.VMEM((fe, d), BF16),
                pltpu.VMEM((2, d, chunk), F32), pltpu.VMEM((2, d, chunk), F32), pltpu.VMEM((2, chunk, d), F32),
                pltpu.SemaphoreType.DMA((3, 2)),
            ],
        ),
        out_shape=jax.ShapeDtypeStruct((n_rows, d), F32),
        compiler_params=_params("arbitrary"),
        name="moe_experts",
    )(tile_expert, tile_halves, xb, w_gu, w_down)


def _combine_kernel(dest_ref, dest_next_ref, yb_hbm, h_ref, gate_ref, p_ref, wp_ref, wg_ref, g_ref, o_ref,
                    y_ref, sems, *, tm):
    i = pl.program_id(0)
    slot = i % 2

    def gather(d_ref, s):
        def issue(r, carry):
            for k in range(TOP_K):
                pltpu.make_async_copy(yb_hbm.at[pl.ds(d_ref[0, 0, r * TOP_K + k], 1)],
                                      y_ref.at[s, k, pl.ds(r, 1)], sems.at[s]).start()
            return carry
        lax.fori_loop(0, tm, issue, 0, unroll=8)

    pl.when(i == 0)(lambda: gather(dest_ref, 0))
    pl.when(i + 1 < pl.num_programs(0))(lambda: gather(dest_next_ref, 1 - slot))
    for k in range(TOP_K):
        pltpu.make_async_copy(yb_hbm.at[pl.ds(0, tm)], y_ref.at[slot, k], sems.at[slot]).wait()

    gates = gate_ref[...]
    h2 = h_ref[...] + (y_ref[slot, 0] * gates[:, 0:1] + y_ref[slot, 1] * gates[:, 1:2])
    h3 = _ple(h2, p_ref[...], wp_ref[...], wg_ref[...])
    o_ref[...] = h3 * _rstd(h3) * g_ref[...]


def _combine(dest3, yb, h, gates, p, wp, wg, g, *, tm):
    t, d = h.shape
    dp = p.shape[2]
    n = t // tm
    row = lambda i: (i, 0)
    const = lambda i: (0, 0)
    dest_spec = lambda index_map: pl.BlockSpec((1, 1, tm * TOP_K), index_map, memory_space=pltpu.SMEM)
    return pl.pallas_call(
        functools.partial(_combine_kernel, tm=tm),
        grid=(n,),
        in_specs=[
            dest_spec(lambda i: (i, 0, 0)),
            dest_spec(lambda i: (jnp.minimum(i + 1, n - 1), 0, 0)),
            pl.BlockSpec(memory_space=pl.ANY),
            pl.BlockSpec((tm, d), row),
            pl.BlockSpec((tm, LANES), row),
            _layer_rows(p, 1, tm),
            _resident((dp, d), const),
            _resident((d, d), const),
            _resident((1, d), const),
        ],
        out_specs=pl.BlockSpec((tm, d), row),
        out_shape=jax.ShapeDtypeStruct((t, d), F32),
        scratch_shapes=[pltpu.VMEM((2, TOP_K, tm, d), F32), pltpu.SemaphoreType.DMA((2,))],
        compiler_params=_params("arbitrary"),
        name="moe_combine",
    )(dest3, dest3, yb, h, gates, p, wp, wg, g)


def _routing_tables(idx, n_tokens, *, tm):
    e_flat = idx.reshape(-1)
    onehot = (e_flat[:, None] == jnp.arange(N_EXPERTS, dtype=jnp.int32)[None, :]).astype(jnp.int32)
    csum = jnp.cumsum(onehot, axis=0)
    counts = csum[-1]
    padded = (counts + EXPERT_TILE - 1) // EXPERT_TILE * EXPERT_TILE
    pend = jnp.cumsum(padded)
    pstart = pend - padded
    dest = jnp.sum(onehot * (csum - 1 + pstart[None, :]), axis=1)
    n_tiles = (n_tokens * TOP_K) // EXPERT_TILE + N_EXPERTS
    tile_row0 = jnp.arange(n_tiles, dtype=jnp.int32) * EXPERT_TILE
    tile_expert = jnp.minimum(jnp.sum((tile_row0[:, None] >= pend[None, :]).astype(jnp.int32), axis=1),
                              N_EXPERTS - 1)
    tile_rows = jnp.clip(counts[tile_expert] - (tile_row0 - pstart[tile_expert]), 0, EXPERT_TILE)
    tile_halves = ((tile_rows + EXPERT_TILE // 2 - 1) // (EXPERT_TILE // 2)).astype(jnp.int32)
    dest3 = dest.astype(jnp.int32).reshape(n_tokens // tm, 1, tm * TOP_K)
    return dest3, tile_expert, tile_halves, pend.astype(jnp.int32), padded.astype(jnp.int32), n_tiles


def kernel(x, p, attn_norm, ffn_norm, w_in_a, b_f, w_o_a, kv_norm, w_kv, w_q_b, w_o_b, w_gu_dense, w_down_dense, router_w, router_b, w_gu_moe, w_down_moe, w_ple_proj, w_ple_gate, final_norm):
    batch, seq, d = x.shape
    t = batch * seq
    n_heads = d // HEAD_DIM
    x2 = x.reshape(t, d)
    p2 = p.reshape(p.shape[0], t, p.shape[-1])
    bf = lambda w: w.astype(BF16)

    ii = lax.broadcasted_iota(jnp.int32, (CUMSUM_CHUNK, CUMSUM_CHUNK), 0)
    jj = lax.broadcasted_iota(jnp.int32, (CUMSUM_CHUNK, CUMSUM_CHUNK), 1)
    tri_le = (ii <= jj).astype(BF16)
    tri_ge = (ii >= jj).astype(BF16)

    w_in = bf(w_in_a[0])
    qkv, c = _qkvf_proj(x2, attn_norm[0:1], w_in, w_in[:, 3 * d:].T, b_f[0].reshape(n_heads, 1), tri_le, seq=seq)
    o0 = _fox_attention(qkv, c, batch=batch, seq=seq, d_model=d)
    gains = jnp.stack([kv_norm, attn_norm[1]])
    h3, kv, q1 = _layer0_tail(o0, x2, p2, bf(w_o_a[0]), ffn_norm[0:1], bf(w_gu_dense[0]), bf(w_down_dense[0]),
                              bf(w_ple_proj[0]), bf(w_ple_gate[0]), gains, bf(w_kv), bf(w_q_b[0]))

    o1 = _sb_attention(q1, kv, tri_ge, batch=batch, seq=seq, d_model=d)
    rw = jnp.pad(router_w[0], ((0, 0), (0, LANES - N_EXPERTS)))
    rw_hi = bf(rw)
    rw_lo = bf(rw - rw_hi.astype(F32))
    rw = jnp.concatenate([rw_hi, rw_hi, rw_lo], axis=0)
    rb = jnp.pad(router_b[0], (0, LANES - N_EXPERTS)).reshape(1, LANES)
    h4, fn1, idx, gates = _oproj_router(o1, bf(w_o_b[0]), h3, ffn_norm[1:2], rw, rb)

    tm = 1024
    dest3, tile_expert, tile_halves, pend, padded, n_tiles = _routing_tables(idx[:, :TOP_K], t, tm=tm)
    xb = _dispatch(fn1, dest3, pend, padded, n_tiles * EXPERT_TILE, tm=tm)
    yb = _experts(xb, w_gu_moe[0], w_down_moe[0], tile_expert, tile_halves)
    out = _combine(dest3, yb, h4, gates, p2, bf(w_ple_proj[1]), bf(w_ple_gate[1]),
                   final_norm.reshape(1, d), tm=tm)
    return out.reshape(batch, seq, d)
```
